```python
import math
import jax, jax.numpy as jnp
from jax import lax
import numpy as np

D_MODEL = 1024
BATCH = 4
SEQ = 8192
DEPTH = 2

GRID_W = 64
CTX_LEN = 256
HEAD_DIM = 64
N_GROUPS = 4
GROUP_HEADS = D_MODEL // HEAD_DIM // N_GROUPS
GROUP_W = GROUP_HEADS * HEAD_DIM
N_KV = 2
KV_W = N_KV * HEAD_DIM
WINDOW = 128
BLOCK = 128
RET_CHUNK = 128
D_FF = ((8 * D_MODEL // 3 + 255) // 256) * 256
W_RANK = 64
A_RANK = 64
G_RANK = 128
ROPE_BASE = 10000.0
RMS_EPS = 1e-6
GN_EPS = 64e-5
DECAY_SCALE = 0.6065306597126334
N_MOD = 9
PROJ_SIZES = (GROUP_W, KV_W, KV_W,
              GROUP_W, KV_W, KV_W,
              GROUP_W, GROUP_W, GROUP_W, GROUP_W, GROUP_W,
              3 * GROUP_W, G_RANK, W_RANK, W_RANK, A_RANK, A_RANK)
PROJ_COLS = sum(PROJ_SIZES)
RWKV_MIX = 3 * GROUP_W + W_RANK + A_RANK

kernel_name = 'hybrid_parallel_group_flow_block'


def rms_norm(x, g):
    xf = x.astype(jnp.float32)
    y = xf * lax.rsqrt(jnp.mean(xf * xf, -1, keepdims=True) + RMS_EPS)
    return (y * g.astype(jnp.float32)).astype(x.dtype)


def head_norm(y, g, b=None):
    yf = y.astype(jnp.float32)
    mu = jnp.mean(yf, -1, keepdims=True)
    yn = (yf - mu) * lax.rsqrt(jnp.mean(jnp.square(yf - mu), -1, keepdims=True) + GN_EPS)
    yn = yn.reshape(y.shape[:-2] + (y.shape[-2] * y.shape[-1],)) * g
    return yn if b is None else yn + b


def modulate(h, shift, scale):
    return h * (1.0 + scale) + shift


def swiglu(h, w_in, w_out):
    u = h @ w_in
    return (jax.nn.silu(u[..., :D_FF]) * u[..., D_FF:]) @ w_out


def heads(t):
    return t.reshape(t.shape[:-1] + (t.shape[-1] // HEAD_DIM, HEAD_DIM))


def flip(t):
    return jnp.flip(t, axis=1)


def split_cols(p):
    out, o = [], 0
    for s in PROJ_SIZES:
        out.append(p[..., o:o + s])
        o += s
    return out


def rope_cos_sin(pos, dim):
    inv = 1.0 / (ROPE_BASE ** (jnp.arange(0, dim, 2, dtype=jnp.float32) / dim))
    ang = pos.astype(jnp.float32)[:, None] * inv[None, :]
    return jnp.cos(ang), jnp.sin(ang)


def apply_rope(x, cos, sin):
    half = x.shape[-1] // 2
    xf = x.astype(jnp.float32)
    x1, x2 = xf[..., :half], xf[..., half:]
    c, s = cos[None, :, None, :], sin[None, :, None, :]
    return jnp.concatenate([x1 * c - x2 * s, x2 * c + x1 * s], -1).astype(x.dtype)


def apply_axial_rope(x, rope_row, rope_col):
    half = x.shape[-1] // 2
    return jnp.concatenate([apply_rope(x[..., :half], *rope_row), apply_rope(x[..., half:], *rope_col)], -1)


def dense_attn(q, k, v, sink=None):
    b_, tq, h, hd = q.shape
    g = h // N_KV
    qg = q.reshape(b_, tq, N_KV, g, hd)
    s = jnp.einsum('bqkgd,bskd->bkgqs', qg, k).astype(jnp.float32) * hd ** -0.5
    if sink is not None:
        sk = jnp.broadcast_to(sink.reshape(N_KV, g)[None, :, :, None, None].astype(jnp.float32), s.shape[:-1] + (1,))
        s = jnp.concatenate([s, sk], -1)
    p = jax.nn.softmax(s, axis=-1)[..., :k.shape[1]]
    o = jnp.einsum('bkgqs,bskd->bqkgd', p.astype(v.dtype), v)
    return o.reshape(b_, tq, h * hd)


def window_attn_latent(q, k, v, kc, vc, sink):
    b_, n, h, hd = q.shape
    nb = n // BLOCK
    g = h // N_KV
    qb = q.reshape(b_, nb, BLOCK, N_KV, g, hd)

    def band(t):
        tp = jnp.pad(t, ((0, 0), (BLOCK, BLOCK), (0, 0), (0, 0))).reshape(b_, nb + 2, BLOCK, N_KV, hd)
        return jnp.concatenate([tp[:, :-2], tp[:, 1:-1], tp[:, 2:]], axis=2)

    kw, vw = band(k), band(v)
    scale = hd ** -0.5
    s_win = jnp.einsum('bnqkgd,bnskd->bnkgqs', qb, kw).astype(jnp.float32) * scale
    s_ctx = jnp.einsum('bnqkgd,bskd->bnkgqs', qb, kc).astype(jnp.float32) * scale
    qpos = jnp.arange(nb)[:, None, None] * BLOCK + jnp.arange(BLOCK)[None, :, None]
    kpos = jnp.arange(nb)[:, None, None] * BLOCK - BLOCK + jnp.arange(3 * BLOCK)[None, None, :]
    valid = (jnp.abs(kpos - qpos) <= WINDOW) & (kpos >= 0) & (kpos < n)
    s_win = jnp.where(valid[None, :, None, None], s_win, -jnp.inf)
    sk = jnp.broadcast_to(sink.reshape(N_KV, g)[None, None, :, :, None, None].astype(jnp.float32), s_ctx.shape[:-1] + (1,))
    p = jax.nn.softmax(jnp.concatenate([s_win, s_ctx, sk], -1), axis=-1)
    w3 = 3 * BLOCK
    o = (jnp.einsum('bnkgqs,bnskd->bnqkgd', p[..., :w3].astype(v.dtype), vw)
         + jnp.einsum('bnkgqs,bskd->bnqkgd', p[..., w3:w3 + kc.shape[1]].astype(vc.dtype), vc))
    return o.reshape(b_, n, h * hd)


def global_attn_latent(q, k_all, v_all):
    b_, n, h, hd = q.shape
    nb = n // BLOCK
    qb = jnp.moveaxis(q.reshape(b_, nb, BLOCK, h, hd), 1, 0)
    ob = lax.map(lambda qi: dense_attn(qi, k_all, v_all), qb)
    return jnp.moveaxis(ob, 0, 1).reshape(b_, n, h * hd)


def retention_chunked(q, k, v, log_gamma, s0):
    b_, t, h, dk = q.shape
    c = RET_CHUNK
    nc = t // c
    qc = q.astype(jnp.float32).reshape(b_, nc, c, h, dk)
    kc = k.astype(jnp.float32).reshape(b_, nc, c, h, dk)
    vc = v.astype(jnp.float32).reshape(b_, nc, c, h, -1)
    idx = jnp.arange(c, dtype=jnp.float32)
    rel = idx[:, None] - idx[None, :]
    d_intra = jnp.where(rel[None] >= 0, jnp.exp(jnp.maximum(rel, 0.0)[None] * log_gamma[:, None, None]), 0.0)
    att = jnp.einsum('bnihd,bnjhd->bnhij', qc, kc) * d_intra[None, None]
    o_intra = jnp.einsum('bnhij,bnjhe->bnihe', att, vc)
    k_w = jnp.exp((c - 1 - idx)[None, :] * log_gamma[:, None])
    u = jnp.einsum('bnjhd,hj,bnjhe->nbhde', kc, k_w, vc)
    chunk_decay = jnp.exp(c * log_gamma)[None, :, None, None]

    def step(s, u_n):
        return chunk_decay * s + u_n, s

    s_final, s_prev = lax.scan(step, s0, u)
    q_w = jnp.exp((idx + 1)[None, :] * log_gamma[:, None])
    o_cross = jnp.einsum('bnihd,hi,nbhde->bnihe', qc, q_w, s_prev)
    return (o_intra + o_cross).reshape(b_, t, h, -1), s_final


def token_shift(z, reverse):
    if reverse:
        return jnp.pad(z[:, 1:], ((0, 0), (0, 1), (0, 0)))
    return jnp.pad(z[:, :-1], ((0, 0), (1, 0), (0, 0)))


def rwkv7_scan(r, w, k, v, a, b, s0, reverse):
    def step(s, inp):
        r_t, w_t, k_t, v_t, a_t, b_t = inp
        sa = jnp.einsum('bhij,bhj->bhi', s, a_t)
        s = s * w_t[:, :, None, :] + sa[..., None] * b_t[:, :, None, :] + v_t[..., None] * k_t[:, :, None, :]
        return s, jnp.einsum('bhij,bhj->bhi', s, r_t)

    xs = tuple(jnp.moveaxis(t, 1, 0) for t in (r, w, k, v, a, b))
    s_final, y = lax.scan(step, s0, xs, reverse=reverse)
    return jnp.moveaxis(y, 0, 1), s_final


def rwkv7_direction(rkv, w_low, a_low, s0, reverse, mu, w0, w2, a0, a2, rho, k_k, k_a, ln_g, ln_b):
    z = jnp.concatenate([rkv, w_low, a_low], -1).astype(jnp.float32)
    z = z + mu * (token_shift(z, reverse) - z)
    r, k, v = z[..., :GROUP_W], z[..., GROUP_W:2 * GROUP_W], z[..., 2 * GROUP_W:3 * GROUP_W]
    w_in_ = z[..., 3 * GROUP_W:3 * GROUP_W + W_RANK]
    a_in = z[..., 3 * GROUP_W + W_RANK:]
    decay = jnp.exp(-DECAY_SCALE * jax.nn.sigmoid(w0 + jnp.tanh(w_in_) @ w2))
    a = jax.nn.sigmoid(a0 + a_in @ a2)
    kk = heads(k * k_k)
    kk = kk / jnp.maximum(jnp.sqrt(jnp.sum(kk * kk, -1, keepdims=True)), 1e-12)
    k_t = k * (1.0 + (a - 1.0) * k_a)
    r_h, k_h, v_h = heads(r), heads(k_t), heads(v)
    y, s_final = rwkv7_scan(r_h, heads(decay), k_h, v_h, -kk, kk * heads(a), s0, reverse)
    bonus = jnp.sum(r_h * k_h * rho, -1, keepdims=True) * v_h
    y = head_norm(y, ln_g, ln_b) + bonus.reshape(bonus.shape[:2] + (GROUP_W,))
    return y, s_final


def mixer(h, hc, w_in_l, w_out_l, sink, qk_g, ret_g, mu, w0, w2, a0, a2, rho, k_k, k_a, g2, ln_g, ln_b,
          rope_row, rope_col, rope_seq, log_gamma, need_ctx_out):
    (qa, ka, va, qb, kb, vb, qr, kr, vr, grf, grb, rkv, gd, wdf, wdb, adf, adb) = split_cols(h @ w_in_l)
    (cqa, cka, cva, cqb, ckb, cvb, cqr, ckr, cvr, cgrf, cgrb, crkv, cgd, cwdf, cwdb, cadf, cadb) = split_cols(hc @ w_in_l)
    b_ = h.shape[0]
    cka_h, cva_h = heads(cka), heads(cva)
    out_a = window_attn_latent(apply_axial_rope(heads(qa), rope_row, rope_col),
                               apply_axial_rope(heads(ka), rope_row, rope_col), heads(va), cka_h, cva_h, sink)
    ckb_h, cvb_h = rms_norm(heads(ckb), qk_g[1]), heads(cvb)
    qb_h = apply_axial_rope(rms_norm(heads(qb), qk_g[0]), rope_row, rope_col)
    kb_h = apply_axial_rope(rms_norm(heads(kb), qk_g[1]), rope_row, rope_col)
    out_b = global_attn_latent(qb_h, jnp.concatenate([kb_h, ckb_h], 1), jnp.concatenate([heads(vb), cvb_h], 1))
    zeros = jnp.zeros((b_, GROUP_HEADS, HEAD_DIM, HEAD_DIM), jnp.float32)
    k_scale = HEAD_DIM ** -0.5
    cq_r, ck_r, cv_r = heads(cqr), heads(ckr) * k_scale, heads(cvr)
    oc_f, sc_f = retention_chunked(cq_r, ck_r, cv_r, log_gamma, zeros)
    oc_b, sc_b = retention_chunked(flip(cq_r), flip(ck_r), flip(cv_r), log_gamma, zeros)
    q_r = apply_rope(heads(qr), *rope_seq)
    k_r = apply_rope(heads(kr), *rope_seq) * k_scale
    v_r = heads(vr)
    o_f, _ = retention_chunked(q_r, k_r, v_r, log_gamma, sc_f)
    o_b, _ = retention_chunked(flip(q_r), flip(k_r), flip(v_r), log_gamma, sc_b)
    out_c = head_norm(o_f, ret_g) * jax.nn.silu(grf) + head_norm(flip(o_b), ret_g) * jax.nn.silu(grb)
    yc_f, sdc_f = rwkv7_direction(crkv, cwdf, cadf, zeros, False, mu[0], w0[0], w2[0], a0[0], a2[0], rho[0], k_k, k_a, ln_g, ln_b)
    yc_b, sdc_b = rwkv7_direction(crkv, cwdb, cadb, zeros, True, mu[1], w0[1], w2[1], a0[1], a2[1], rho[1], k_k, k_a, ln_g, ln_b)
    y_f, _ = rwkv7_direction(rkv, wdf, adf, sdc_f, False, mu[0], w0[0], w2[0], a0[0], a2[0], rho[0], k_k, k_a, ln_g, ln_b)
    y_b, _ = rwkv7_direction(rkv, wdb, adb, sdc_b, True, mu[1], w0[1], w2[1], a0[1], a2[1], rho[1], k_k, k_a, ln_g, ln_b)
    out_d = (y_f + y_b) * (jax.nn.sigmoid(gd) @ g2)
    dt = h.dtype
    out = jnp.concatenate([out_a, out_b, out_c.astype(dt), out_d.astype(dt)], -1) @ w_out_l
    if not need_ctx_out:
        return out, None
    oac = dense_attn(heads(cqa), cka_h, cva_h, sink)
    obc = dense_attn(rms_norm(heads(cqb), qk_g[0]), ckb_h, cvb_h)
    occ = head_norm(oc_f, ret_g) * jax.nn.silu(cgrf) + head_norm(flip(oc_b), ret_g) * jax.nn.silu(cgrb)
    odc = (yc_f + yc_b) * (jax.nn.sigmoid(cgd) @ g2)
    out_ctx = jnp.concatenate([oac, obc, occ.astype(dt), odc.astype(dt)], -1) @ w_out_l
    return out, out_ctx


def setup_inputs(seed: int = 0) -> dict:
    key = jax.random.key(seed)
    ks = jax.random.split(key, 26)
    D = D_MODEL
    L = DEPTH

    def nrm(k, shape, s):
        return jax.random.normal(k, shape, jnp.float32) * s

    return {
        'x': nrm(ks[0], (BATCH, SEQ, D), 1.0),
        'c': nrm(ks[1], (BATCH, D), 1.0),
        'ctx': nrm(ks[2], (BATCH, CTX_LEN, D), 1.0),
        'c_ctx': nrm(ks[3], (D,), 1.0),
        'w_mod': nrm(ks[4], (L, D, N_MOD * D), 0.5 * D ** -0.5),
        'b_mod': nrm(ks[5], (L, N_MOD * D), 0.01),
        'norm_g': 1.0 + nrm(ks[6], (L, 3, D), 0.02),
        'ffn_w_in': nrm(ks[7], (L, 2, D, 2 * D_FF), D ** -0.5),
        'ffn_w_out': nrm(ks[8], (L, 2, D_FF, D), D_FF ** -0.5),
        'w_in': nrm(ks[9], (L, D, PROJ_COLS), D ** -0.5),
        'w_out': nrm(ks[10], (L, N_GROUPS * GROUP_W, D), (N_GROUPS * GROUP_W) ** -0.5),
        'attn_sink': nrm(ks[11], (L, GROUP_HEADS), 1.0),
        'qk_norm_g': 1.0 + nrm(ks[12], (L, 2, HEAD_DIM), 0.02),
        'ret_norm_g': 1.0 + nrm(ks[13], (L, GROUP_W), 0.02),
        'rwkv_mu': jax.random.uniform(ks[14], (L, 2, RWKV_MIX), jnp.float32),
        'rwkv_w0': jax.random.uniform(ks[15], (L, 2, GROUP_W), jnp.float32, -4.0, 2.0),
        'rwkv_w2': nrm(ks[16], (L, 2, W_RANK, GROUP_W), 0.1 * W_RANK ** -0.5),
        'rwkv_a0': nrm(ks[17], (L, 2, GROUP_W), 0.5),
        'rwkv_a2': nrm(ks[18], (L, 2, A_RANK, GROUP_W), 0.1 * A_RANK ** -0.5),
        'rwkv_rho': nrm(ks[19], (L, 2, GROUP_HEADS, HEAD_DIM), 0.1),
        'rwkv_k_k': 1.0 + nrm(ks[20], (L, GROUP_W), 0.1),
        'rwkv_k_a': 1.0 + nrm(ks[21], (L, GROUP_W), 0.1),
        'rwkv_g2': nrm(ks[22], (L, G_RANK, GROUP_W), G_RANK ** -0.5),
        'rwkv_ln_g': 1.0 + nrm(ks[23], (L, GROUP_W), 0.02),
        'rwkv_ln_b': nrm(ks[24], (L, GROUP_W), 0.01),
        'final_norm_g': 1.0 + nrm(ks[25], (D,), 0.02),
    }


def reference(x, c, ctx, c_ctx, w_mod, b_mod, norm_g, ffn_w_in, ffn_w_out, w_in, w_out, attn_sink, qk_norm_g,
              ret_norm_g, rwkv_mu, rwkv_w0, rwkv_w2, rwkv_a0, rwkv_a2, rwkv_rho, rwkv_k_k, rwkv_k_a, rwkv_g2,
              rwkv_ln_g, rwkv_ln_b, final_norm_g):
    n_tok = x.shape[1]
    rows = n_tok // GRID_W
    row = jnp.repeat(jnp.arange(rows), GRID_W)
    col = jnp.arange(rows * GRID_W) % GRID_W
    rope_row = rope_cos_sin(row, HEAD_DIM // 2)
    rope_col = rope_cos_sin(col, HEAD_DIM // 2)
    rope_seq = rope_cos_sin(jnp.arange(n_tok), HEAD_DIM)
    log_gamma = jnp.log1p(-jnp.exp2(-5.0 - jnp.arange(GROUP_HEADS, dtype=jnp.float32)))
    xc = ctx
    for l in range(DEPTH):
        need_ctx_out = l < DEPTH - 1
        mod = (jax.nn.silu(c) @ w_mod[l] + b_mod[l]).reshape(c.shape[0], N_MOD, 1, D_MODEL)
        mod_c = (jax.nn.silu(c_ctx) @ w_mod[l] + b_mod[l]).reshape(N_MOD, 1, D_MODEL)
        x = x + 0.5 * mod[:, 2] * swiglu(modulate(rms_norm(x, norm_g[l, 0]), mod[:, 0], mod[:, 1]), ffn_w_in[l, 0], ffn_w_out[l, 0])
        xc = xc + 0.5 * mod_c[2] * swiglu(modulate(rms_norm(xc, norm_g[l, 0]), mod_c[0], mod_c[1]), ffn_w_in[l, 0], ffn_w_out[l, 0])
        h = modulate(rms_norm(x, norm_g[l, 1]), mod[:, 3], mod[:, 4])
        hc = modulate(rms_norm(xc, norm_g[l, 1]), mod_c[3], mod_c[4])
        y, yc = mixer(h, hc, w_in[l], w_out[l], attn_sink[l], qk_norm_g[l], ret_norm_g[l], rwkv_mu[l], rwkv_w0[l],
                      rwkv_w2[l], rwkv_a0[l], rwkv_a2[l], rwkv_rho[l], rwkv_k_k[l], rwkv_k_a[l], rwkv_g2[l],
                      rwkv_ln_g[l], rwkv_ln_b[l], rope_row, rope_col, rope_seq, log_gamma, need_ctx_out)
        x = x + mod[:, 5] * y
        x = x + 0.5 * mod[:, 8] * swiglu(modulate(rms_norm(x, norm_g[l, 2]), mod[:, 6], mod[:, 7]), ffn_w_in[l, 1], ffn_w_out[l, 1])
        if need_ctx_out:
            xc = xc + mod_c[5] * yc
            xc = xc + 0.5 * mod_c[8] * swiglu(modulate(rms_norm(xc, norm_g[l, 2]), mod_c[6], mod_c[7]), ffn_w_in[l, 1], ffn_w_out[l, 1])
    return rms_norm(x, final_norm_g)
```

```python
import functools

import numpy as np
import jax
import jax.numpy as jnp
from jax import lax
from jax.experimental import pallas as pl
from jax.experimental.pallas import tpu as pltpu

F32 = jnp.float32
BF16 = jnp.bfloat16

D_MODEL = 1024
HEAD_DIM = 64
GROUP_W = 256
KV_W = 128
GRID_W = 64
BLOCK = 128
D_FF = 2816
W_RANK = 64
A_RANK = 64
G_RANK = 128
N_MOD = 9
ROPE_BASE = 10000.0
RMS_EPS = 1e-6
GN_EPS = 64e-5
DECAY_SCALE = 0.6065306597126334
PROJ_COLS = 3456
LANES = 128
MOD_ROWS = 8

ROW_TILE = 256
FF_CHUNK = 1408
Q_TILE = 256
KEY_CHUNK = 1024
RET_CHUNK = 256
RWKV_CHUNK = 64
VMEM_LIMIT = 56 * 1024 * 1024

NEG_BIG = -1e30
HIGHEST = lax.Precision.HIGHEST


def _dot(a, b):
  return jnp.dot(a.astype(BF16), b.astype(BF16), preferred_element_type=F32)


def _dot_nt(a, b):
  return lax.dot_general(a.astype(BF16), b.astype(BF16), (((1,), (1,)), ((), ())),
                         preferred_element_type=F32)


def _dot_tn(a, b):
  return jnp.dot(a.astype(F32).T.astype(BF16), b.astype(BF16), preferred_element_type=F32)


def _silu(x):
  return x * jax.nn.sigmoid(x)


def _lane_lo(shape):
  return (lax.broadcasted_iota(jnp.int32, shape, len(shape) - 1) % LANES) < HEAD_DIM


def _half_sum(x, lo):
  s_lo = jnp.sum(jnp.where(lo, x, 0.0), axis=-1, keepdims=True)
  s_all = jnp.sum(x, axis=-1, keepdims=True)
  return jnp.where(lo, s_lo, s_all - s_lo)


def _head_norm_slab(y, lo):
  mu = _half_sum(y, lo) * (1.0 / HEAD_DIM)
  d = y - mu
  var = _half_sum(d * d, lo) * (1.0 / HEAD_DIM)
  return d * lax.rsqrt(var + GN_EPS)


def _rope_slab(x, c, s, half):
  first = (lax.broadcasted_iota(jnp.int32, x.shape, 1) % (2 * half)) < half
  partner = jnp.where(first, pltpu.roll(x, LANES - half, axis=1), pltpu.roll(x, half, axis=1))
  return x * c + partner * s


def _mod_kernel(c_ref, w_ref, b_ref, o_ref):
  s = _silu(c_ref[...])
  o_ref[...] = jnp.dot(s, w_ref[...], precision=HIGHEST, preferred_element_type=F32) + b_ref[...]


def _modulation(cs, w_mod, b_mod):
  depth = w_mod.shape[0]
  return pl.pallas_call(
      _mod_kernel,
      grid=(depth, N_MOD),
      in_specs=[
          pl.BlockSpec((MOD_ROWS, D_MODEL), lambda l, j: (0, 0)),
          pl.BlockSpec((None, D_MODEL, D_MODEL), lambda l, j: (l, 0, j)),
          pl.BlockSpec((None, 1, D_MODEL), lambda l, j: (l, 0, j)),
      ],
      out_specs=pl.BlockSpec((None, MOD_ROWS, D_MODEL), lambda l, j: (l, 0, j)),
      out_shape=jax.ShapeDtypeStruct((depth, MOD_ROWS, N_MOD * D_MODEL), F32),
      compiler_params=pltpu.CompilerParams(
          dimension_semantics=("arbitrary", "arbitrary"), vmem_limit_bytes=VMEM_LIMIT),
      name="modulation",
  )(cs, w_mod, b_mod.reshape(depth, 1, N_MOD * D_MODEL))


def _rms_mod(x, g, shift, scale):
  h = x * lax.rsqrt(jnp.mean(x * x, axis=-1, keepdims=True) + RMS_EPS) * g
  return h * (1.0 + scale) + shift


def _swiglu(h, w1_ref, w2_ref):
  hb = h.astype(BF16)
  acc = None
  for c0 in range(0, D_FF, FF_CHUNK):
    u1 = jnp.dot(hb, w1_ref[:, c0:c0 + FF_CHUNK], preferred_element_type=F32)
    u2 = jnp.dot(hb, w1_ref[:, D_FF + c0:D_FF + c0 + FF_CHUNK], preferred_element_type=F32)
    a = (_silu(u1) * u2).astype(BF16)
    part = jnp.dot(a, w2_ref[c0:c0 + FF_CHUNK, :], preferred_element_type=F32)
    acc = part if acc is None else acc + part
  return acc


def _head_rms(x, g):
  w = x.shape[-1]
  r = lax.broadcasted_iota(jnp.int32, (w, w), 0) // HEAD_DIM
  c = lax.broadcasted_iota(jnp.int32, (w, w), 1) // HEAD_DIM
  avg = jnp.where(r == c, 1.0 / HEAD_DIM, 0.0)
  ms = jnp.dot(x * x, avg, precision=HIGHEST, preferred_element_type=F32)
  return x * lax.rsqrt(ms + RMS_EPS) * g


def _rope_wide(x, c, s, half):
  parts = [_rope_slab(x[:, o:o + LANES], c, s, half) for o in range(0, x.shape[-1], LANES)]
  return parts[0] if len(parts) == 1 else jnp.concatenate(parts, axis=-1)


def _ffn_proj_kernel(x_ref, mod_ref, ng_ref, w1_ref, w2_ref, win_ref, qkg_ref,
                     cax_ref, sax_ref, csq_ref, ssq_ref,
                     xo_ref, qa_ref, kva_ref, qb_ref, kvb_ref, rc_ref, gc_ref, dd_ref):
  x = x_ref[...]
  m = mod_ref[...]
  ng = ng_ref[...]
  h = _rms_mod(x, ng[0:1], m[0:1], m[1:2])
  x1 = x + 0.5 * m[2:3] * _swiglu(h, w1_ref, w2_ref)
  xo_ref[...] = x1
  hb = _rms_mod(x1, ng[1:2], m[3:4], m[4:5]).astype(BF16)

  cax, sax = cax_ref[...], sax_ref[...]
  csq, ssq = csq_ref[...], ssq_ref[...]
  qkg = qkg_ref[...]
  scale = HEAD_DIM ** -0.5

  p = jnp.dot(hb, win_ref[:, 0:512], preferred_element_type=F32)
  qa_ref[...] = (_rope_wide(p[:, 0:256], cax, sax, 16) * scale).astype(BF16)
  kva_ref[:, 0:128] = _rope_wide(p[:, 256:384], cax, sax, 16).astype(BF16)
  kva_ref[:, 128:256] = p[:, 384:512].astype(BF16)
  p = jnp.dot(hb, win_ref[:, 512:1024], preferred_element_type=F32)
  qn = _head_rms(p[:, 0:256], qkg[0:1, :])
  kn = _head_rms(p[:, 256:384], qkg[1:2, 0:128])
  qb_ref[...] = (_rope_wide(qn, cax, sax, 16) * scale).astype(BF16)
  kvb_ref[:, 0:128] = _rope_wide(kn, cax, sax, 16).astype(BF16)
  kvb_ref[:, 128:256] = p[:, 384:512].astype(BF16)
  p = jnp.dot(hb, win_ref[:, 1024:2304], preferred_element_type=F32)
  rc_ref[:, 0:256] = _rope_wide(p[:, 0:256], csq, ssq, 32).astype(BF16)
  rc_ref[:, 256:512] = (_rope_wide(p[:, 256:512], csq, ssq, 32) * scale).astype(BF16)
  rc_ref[:, 512:768] = p[:, 512:768].astype(BF16)
  gc_ref[...] = p[:, 768:1280]
  dd_ref[...] = jnp.dot(hb, win_ref[:, 2304:3456], preferred_element_type=F32)


def _ffn_proj(x, mod, ng, w1, w2, win, qkg, tabs, dims):
  b, n, ct = dims
  rows = x.shape[0]
  tm = ROW_TILE
  tpb = n // tm
  nlt = b * tpb

  def row(i):
    return (i, 0)

  def const(i):
    return (0, 0)

  def mod_idx(i):
    return (jnp.minimum(i // tpb, b), 0, 0)

  def tab_idx(i):
    return (jnp.where(i < nlt, i % tpb, tpb), 0)

  widths = (D_MODEL, 256, 256, 256, 256, 768, 512, 1152)
  dtypes = (F32, BF16, BF16, BF16, BF16, BF16, F32, F32)
  return pl.pallas_call(
      _ffn_proj_kernel,
      grid=(rows // tm,),
      in_specs=[
          pl.BlockSpec((tm, D_MODEL), row),
          pl.BlockSpec((None, N_MOD, D_MODEL), mod_idx),
          pl.BlockSpec((3, D_MODEL), const),
          pl.BlockSpec((D_MODEL, 2 * D_FF), const),
          pl.BlockSpec((D_FF, D_MODEL), const),
          pl.BlockSpec((D_MODEL, PROJ_COLS), const),
          pl.BlockSpec((2, GROUP_W), const),
          pl.BlockSpec((tm, LANES), tab_idx),
          pl.BlockSpec((tm, LANES), tab_idx),
          pl.BlockSpec((tm, LANES), tab_idx),
          pl.BlockSpec((tm, LANES), tab_idx),
      ],
      out_specs=[pl.BlockSpec((tm, w), row) for w in widths],
      out_shape=[jax.ShapeDtypeStruct((rows, w), dt) for w, dt in zip(widths, dtypes)],
      compiler_params=pltpu.CompilerParams(
          dimension_semantics=("arbitrary",), vmem_limit_bytes=VMEM_LIMIT),
      name="ffn_proj",
  )(x, mod, ng, w1, w2, win, qkg, *tabs)


def _out_ffn_kernel(x_ref, mod_ref, ng_ref, oa_ref, ob_ref, cf_ref, cb_ref, yf_ref, yb_ref,
                    gd_ref, g2_ref, wo_ref, w1_ref, w2_ref, fg_ref, xo_ref, *, final):
  x = x_ref[...]
  m = mod_ref[...]
  ng = ng_ref[...]
  gate = _dot(jax.nn.sigmoid(gd_ref[...]), g2_ref[...])
  od = (yf_ref[...] + yb_ref[...]) * gate
  oc = cf_ref[...] + cb_ref[...]
  y = jnp.dot(oa_ref[...], wo_ref[0:256, :], preferred_element_type=F32)
  y = y + jnp.dot(ob_ref[...], wo_ref[256:512, :], preferred_element_type=F32)
  y = y + jnp.dot(oc.astype(BF16), wo_ref[512:768, :], preferred_element_type=F32)
  y = y + jnp.dot(od.astype(BF16), wo_ref[768:1024, :], preferred_element_type=F32)
  x1 = x + m[5:6] * y
  h = _rms_mod(x1, ng[2:3], m[6:7], m[7:8])
  x2 = x1 + 0.5 * m[8:9] * _swiglu(h, w1_ref, w2_ref)
  if final:
    x2 = x2 * lax.rsqrt(jnp.mean(x2 * x2, axis=-1, keepdims=True) + RMS_EPS) * fg_ref[...]
  xo_ref[...] = x2


def _out_ffn(x, mod, ng, oa, ob, cf, cb, yf, yb, dd, g2, wo, w1, w2, fg, dims, final):
  b, n, ct = dims
  tm = ROW_TILE
  tpb = n // tm
  rows = b * n if final else x.shape[0]

  def row(i):
    return (i, 0)

  def const(i):
    return (0, 0)

  def mod_idx(i):
    return (jnp.minimum(i // tpb, b), 0, 0)

  return pl.pallas_call(
      functools.partial(_out_ffn_kernel, final=final),
      grid=(rows // tm,),
      in_specs=[
          pl.BlockSpec((tm, D_MODEL), row),
          pl.BlockSpec((None, N_MOD, D_MODEL), mod_idx),
          pl.BlockSpec((3, D_MODEL), const),
          pl.BlockSpec((tm, GROUP_W), row),
          pl.BlockSpec((tm, GROUP_W), row),
          pl.BlockSpec((tm, GROUP_W), row),
          pl.BlockSpec((tm, GROUP_W), row),
          pl.BlockSpec((tm, GROUP_W), row),
          pl.BlockSpec((tm, GROUP_W), row),
          pl.BlockSpec((tm, G_RANK), lambda i: (i, 768 // G_RANK)),
          pl.BlockSpec((G_RANK, GROUP_W), const),
          pl.BlockSpec((D_MODEL, D_MODEL), const),
          pl.BlockSpec((D_MODEL, 2 * D_FF), const),
          pl.BlockSpec((D_FF, D_MODEL), const),
          pl.BlockSpec((1, D_MODEL), const),
      ],
      out_specs=pl.BlockSpec((tm, D_MODEL), row),
      out_shape=jax.ShapeDtypeStruct((rows, D_MODEL), F32),
      compiler_params=pltpu.CompilerParams(
          dimension_semantics=("arbitrary",), vmem_limit_bytes=VMEM_LIMIT),
      name="out_ffn_final" if final else "out_ffn",
  )(x, mod, ng, oa, ob, cf, cb, yf, yb, dd, g2, wo, w1, w2, fg)


def _softmax_heads(q, segs, sink_ref, o_ref):
  lo = _lane_lo((1, LANES))
  for s in range(2):
    qs = q[:, s * LANES:(s + 1) * LANES]
    halves = []
    for hf in range(2):
      qm = jnp.where(lo if hf == 0 else jnp.logical_not(lo), qs, jnp.zeros_like(qs))
      scores = []
      for kv, valid in segs:
        sc = _dot_nt(qm, kv[:, 0:LANES])
        if valid is not None:
          sc = jnp.where(valid, sc, NEG_BIG)
        scores.append(sc)
      mx = scores[0].max(axis=-1, keepdims=True)
      for sc in scores[1:]:
        mx = jnp.maximum(mx, sc.max(axis=-1, keepdims=True))
      if sink_ref is not None:
        snk = sink_ref[2 * s + hf:2 * s + hf + 1, 0:1]
        mx = jnp.maximum(mx, snk)
        den = jnp.exp(snk - mx)
      else:
        den = jnp.zeros_like(mx)
      acc = None
      for sc, (kv, _) in zip(scores, segs):
        p = jnp.exp(sc - mx)
        den = den + p.sum(axis=-1, keepdims=True)
        pv = jnp.dot(p.astype(BF16), kv[:, LANES:2 * LANES], preferred_element_type=F32)
        acc = pv if acc is None else acc + pv
      halves.append(acc / den)
    o_ref[:, s * LANES:(s + 1) * LANES] = jnp.where(lo, halves[0], halves[1]).astype(BF16)


def _win_attn_kernel(q_ref, kp_ref, kc_ref, kn_ref, kx_ref, sink_ref, o_ref, *, nlb, nb):
  i = pl.program_id(0)
  n = i % nb
  is_lat = i < nlb
  big = 4 * BLOCK
  thr_prev = jnp.where(jnp.logical_and(is_lat, n > 0), 0, big)
  thr_cur = jnp.where(is_lat, -big, big)
  thr_next = jnp.where(jnp.logical_and(is_lat, n < nb - 1), 0, big)
  r = lax.broadcasted_iota(jnp.int32, (BLOCK, BLOCK), 0)
  c = lax.broadcasted_iota(jnp.int32, (BLOCK, BLOCK), 1)
  segs = [
      (kp_ref[...], (c - r) >= thr_prev),
      (kc_ref[...], (c - r) >= thr_cur),
      (kn_ref[...], (r - c) >= thr_next),
      (kx_ref[...], None),
  ]
  _softmax_heads(q_ref[...], segs, sink_ref, o_ref)


def _win_attn(qa, kva, sink_tab, dims):
  b, n, ct = dims
  rows = qa.shape[0]
  nb = n // BLOCK
  nlb = b * nb
  cpb = ct // BLOCK

  def bidx(i):
    return jnp.where(i < nlb, i // nb, (i - nlb) // cpb)

  def prev(i):
    return (jnp.where(jnp.logical_and(i < nlb, i % nb > 0), i - 1, i), 0)

  def nxt(i):
    return (jnp.where(jnp.logical_and(i < nlb, i % nb < nb - 1), i + 1, i), 0)

  return pl.pallas_call(
      functools.partial(_win_attn_kernel, nlb=nlb, nb=nb),
      grid=(rows // BLOCK,),
      in_specs=[
          pl.BlockSpec((BLOCK, GROUP_W), lambda i: (i, 0)),
          pl.BlockSpec((BLOCK, GROUP_W), prev),
          pl.BlockSpec((BLOCK, GROUP_W), lambda i: (i, 0)),
          pl.BlockSpec((BLOCK, GROUP_W), nxt),
          pl.BlockSpec((ct, GROUP_W), lambda i: ((b * n) // ct + bidx(i), 0)),
          pl.BlockSpec((4, LANES), lambda i: (0, 0)),
      ],
      out_specs=pl.BlockSpec((BLOCK, GROUP_W), lambda i: (i, 0)),
      out_shape=jax.ShapeDtypeStruct((rows, GROUP_W), BF16),
      compiler_params=pltpu.CompilerParams(
          dimension_semantics=("arbitrary",), vmem_limit_bytes=VMEM_LIMIT),
      name="window_attn",
  )(qa, kva, kva, kva, kva, sink_tab)


def _glob_attn_kernel(q_ref, kl_ref, kx_ref, o_ref, m_sc, l_sc, acc_sc, *, n_chunks):
  lo = _lane_lo((1, LANES))
  tq = q_ref.shape[0]
  m_sc[...] = jnp.full(m_sc.shape, NEG_BIG, F32)
  l_sc[...] = jnp.zeros(l_sc.shape, F32)
  acc_sc[...] = jnp.zeros(acc_sc.shape, F32)

  def step(kv):
    nk = kv.shape[0]
    k = kv[:, 0:LANES]
    v = kv[:, LANES:2 * LANES]
    for s in range(2):
      qs = q_ref[:, s * LANES:(s + 1) * LANES]
      for hf in range(2):
        h = 2 * s + hf
        qm = jnp.where(lo if hf == 0 else jnp.logical_not(lo), qs, jnp.zeros_like(qs))
        sc = _dot_nt(qm, k)
        m_prev = m_sc[h]
        m_new = jnp.maximum(m_prev, sc.max(axis=-1, keepdims=True))
        alpha = jnp.exp(m_prev - m_new)
        p = jnp.exp(sc - pltpu.repeat(m_new, nk // LANES, axis=1))
        l_sc[h] = alpha * l_sc[h] + p.sum(axis=-1, keepdims=True)
        acc_sc[h] = alpha * acc_sc[h] + jnp.dot(p.astype(BF16), v, preferred_element_type=F32)
        m_sc[h] = m_new

  if n_chunks:
    def body(j, carry):
      start = pl.multiple_of(j * KEY_CHUNK, KEY_CHUNK)
      step(kl_ref[pl.ds(start, KEY_CHUNK), :])
      return carry
    lax.fori_loop(0, n_chunks, body, 0)
  step(kx_ref[...])

  for s in range(2):
    o_lo = acc_sc[2 * s] / l_sc[2 * s]
    o_hi = acc_sc[2 * s + 1] / l_sc[2 * s + 1]
    o_ref[:, s * LANES:(s + 1) * LANES] = jnp.where(lo, o_lo, o_hi).astype(BF16)


def _glob_attn(qb, kvb, dims, latent):
  b, n, ct = dims
  tq = Q_TILE if latent else BLOCK
  if latent:
    grid = (b, n // tq)
    q_map = lambda bi, j: (bi * (n // tq) + j, 0)
    n_chunks = n // KEY_CHUNK
    row0 = 0
    rows = b * n
  else:
    grid = (b, ct // tq)
    q_map = lambda bi, j: ((b * n) // tq + bi * (ct // tq) + j, 0)
    n_chunks = 0
    rows = b * ct
  kl_rows = n if latent else BLOCK
  out = pl.pallas_call(
      functools.partial(_glob_attn_kernel, n_chunks=n_chunks),
      grid=grid,
      in_specs=[
          pl.BlockSpec((tq, GROUP_W), q_map),
          pl.BlockSpec((kl_rows, GROUP_W), (lambda bi, j: (bi, 0)) if latent else (lambda bi, j: (0, 0))),
          pl.BlockSpec((ct, GROUP_W), lambda bi, j: ((b * n) // ct + bi, 0)),
      ],
      out_specs=pl.BlockSpec((tq, GROUP_W), lambda bi, j: (bi * grid[1] + j, 0)),
      out_shape=jax.ShapeDtypeStruct((rows, GROUP_W), BF16),
      scratch_shapes=[pltpu.VMEM((4, tq, LANES), F32) for _ in range(3)],
      compiler_params=pltpu.CompilerParams(
          dimension_semantics=("arbitrary", "arbitrary"), vmem_limit_bytes=VMEM_LIMIT),
      name="global_attn" if latent else "global_attn_ctx",
  )(qb, kvb, kvb)
  return out


def _ret_kernel(rf_ref, rb_ref, gf_ref, gb_ref, dec_ref, lw_ref, ng_ref, of_ref, ob_ref,
                sf_sc, sb_sc):
  j = pl.program_id(1)
  c = rf_ref.shape[0]

  @pl.when(j == 0)
  def _():
    sf_sc[...] = jnp.zeros(sf_sc.shape, F32)
    sb_sc[...] = jnp.zeros(sb_sc.shape, F32)

  lo = _lane_lo((1, LANES))
  rr = lax.broadcasted_iota(jnp.int32, (LANES, LANES), 0) // HEAD_DIM
  cc = lax.broadcasted_iota(jnp.int32, (LANES, LANES), 1) // HEAD_DIM
  blockdiag = rr == cc
  ng = ng_ref[...]

  for d, (r_ref, g_ref, o_ref, s_sc) in enumerate(
      ((rf_ref, gf_ref, of_ref, sf_sc), (rb_ref, gb_ref, ob_ref, sb_sc))):
    for s in range(2):
      sl = slice(s * LANES, (s + 1) * LANES)
      q = r_ref[:, s * LANES:(s + 1) * LANES]
      k = r_ref[:, 256 + s * LANES:256 + (s + 1) * LANES]
      v = r_ref[:, 512 + s * LANES:512 + (s + 1) * LANES]
      qw = lw_ref[d, 0][:, sl]
      kw = lw_ref[d, 1][:, sl]
      cd = lw_ref[d, 2][0:1, sl]
      st = s_sc[s]
      o = _dot(q.astype(F32) * qw, st)
      halves = []
      for hf in range(2):
        qm = jnp.where(lo if hf == 0 else jnp.logical_not(lo), q, jnp.zeros_like(q))
        att = _dot_nt(qm, k) * dec_ref[d, 2 * s + hf]
        halves.append(_dot(att, v))
      o = o + jnp.where(lo, halves[0], halves[1])
      u = _dot_tn(k.astype(F32) * kw, v)
      s_sc[s] = st * cd + jnp.where(blockdiag, u, 0.0)
      gate = g_ref[:, d * GROUP_W + s * LANES:d * GROUP_W + (s + 1) * LANES]
      o_ref[:, sl] = _head_norm_slab(o, lo) * ng[0:1, sl] * _silu(gate)


def _retention(rc, gc, dec, lw, ng, dims):
  b, n, ct = dims
  c = RET_CHUNK
  rows = rc.shape[0]
  ncx = ct // c
  ncl = n // c
  base = (b * n) // c

  def fwd(bi, j):
    return (jnp.where(j < ncx, base + bi * ncx + j, bi * ncl + (j - ncx)), 0)

  def bwd(bi, j):
    return (jnp.where(j < ncx, base + bi * ncx + (ncx - 1 - j), bi * ncl + (ncl - 1 - (j - ncx))), 0)

  return pl.pallas_call(
      _ret_kernel,
      grid=(b, ncx + ncl),
      in_specs=[
          pl.BlockSpec((c, 768), fwd),
          pl.BlockSpec((c, 768), bwd),
          pl.BlockSpec((c, 512), fwd),
          pl.BlockSpec((c, 512), bwd),
          pl.BlockSpec((2, 4, c, c), lambda bi, j: (0, 0, 0, 0)),
          pl.BlockSpec((2, 3, c, GROUP_W), lambda bi, j: (0, 0, 0, 0)),
          pl.BlockSpec((1, GROUP_W), lambda bi, j: (0, 0)),
      ],
      out_specs=[pl.BlockSpec((c, GROUP_W), fwd), pl.BlockSpec((c, GROUP_W), bwd)],
      out_shape=[jax.ShapeDtypeStruct((rows, GROUP_W), F32)] * 2,
      scratch_shapes=[pltpu.VMEM((2, LANES, LANES), F32)] * 2,
      compiler_params=pltpu.CompilerParams(
          dimension_semantics=("arbitrary", "arbitrary"), vmem_limit_bytes=VMEM_LIMIT),
      name="retention",
  )(rc, rc, gc, gc, dec, lw, ng)


def _rwkv_kernel(df_ref, db_ref, mu_ref, w0_ref, w2_ref, a0_ref, a2_ref, rho_ref, kk_ref, ka_ref,
                 lng_ref, lnb_ref, yf_ref, yb_ref, s_sc, z_sc, *, ncx, ncl):
  j = pl.program_id(1)
  c = RWKV_CHUNK
  at_start = jnp.logical_or(j == 0, j == ncx)

  @pl.when(j == 0)
  def _():
    s_sc[...] = jnp.zeros(s_sc.shape, F32)

  @pl.when(at_start)
  def _():
    z_sc[:, 0:8, :] = jnp.zeros((2, 8, z_sc.shape[2]), F32)
    z_sc[:, c + 8:c + 16, :] = jnp.zeros((2, 8, z_sc.shape[2]), F32)

  lo = _lane_lo((1, LANES))
  rr = lax.broadcasted_iota(jnp.int32, (LANES, LANES), 0) // HEAD_DIM
  cc = lax.broadcasted_iota(jnp.int32, (LANES, LANES), 1) // HEAD_DIM
  blockdiag = rr == cc
  ti = lax.broadcasted_iota(jnp.int32, (c, c), 0)
  si = lax.broadcasted_iota(jnp.int32, (c, c), 1)
  eye = jnp.where(ti == si, 1.0, 0.0)
  kkp = kk_ref[...]
  kap = ka_ref[...]
  lng = lng_ref[...]
  lnb = lnb_ref[...]

  for d, (d_ref, y_ref) in enumerate(((df_ref, yf_ref), (db_ref, yb_ref))):
    before = (si < ti) if d == 0 else (si > ti)
    upto = (si <= ti) if d == 0 else (si >= ti)
    z = jnp.concatenate([d_ref[:, 0:768], d_ref[:, 896 + d * LANES:1024 + d * LANES]], axis=-1)
    z_sc[d, 8:c + 8, :] = z
    if d == 0:
      zs = z_sc[d, 7:c + 7, :]
    else:
      zs = z_sc[d, 9:c + 9, :]
    mu = mu_ref[d:d + 1, :]
    zm = z + mu * (zs - z)
    if d == 0:
      z_sc[d, 7:8, :] = z[c - 1:c, :]
    else:
      z_sc[d, c + 8:c + 9, :] = z[0:1, :]

    r = zm[:, 0:256]
    k = zm[:, 256:512]
    v = zm[:, 512:768]
    wa = zm[:, 768:896]
    logw = -DECAY_SCALE * jax.nn.sigmoid(w0_ref[d:d + 1, :] + _dot(jnp.tanh(wa), w2_ref[d]))
    ag = jax.nn.sigmoid(a0_ref[d:d + 1, :] + _dot(wa, a2_ref[d]))
    kkr = k * kkp
    kt = k * (1.0 + (ag - 1.0) * kap)
    rk = r * kt * rho_ref[d:d + 1, :]
    cum = jnp.dot(jnp.where(upto, 1.0, 0.0), logw, precision=HIGHEST, preferred_element_type=F32)
    tot = cum[c - 1:c, :] if d == 0 else cum[0:1, :]
    e_pos = jnp.exp(cum)
    e_neg = jnp.exp(-cum)
    e_prev = jnp.exp(cum - logw)
    e_rest = jnp.exp(tot - cum)
    g_all = jnp.exp(tot)

    for s in range(2):
      sl = slice(s * LANES, (s + 1) * LANES)
      kk_s = kkr[:, sl]
      nrm = jnp.sqrt(_half_sum(kk_s * kk_s, lo))
      kk_s = kk_s / jnp.maximum(nrm, 1e-12)
      a_s = -kk_s
      b_s = kk_s * ag[:, sl]
      kt_s = kt[:, sl]
      r_s = r[:, sl]
      v_s = v[:, sl]
      bonus = _half_sum(rk[:, sl], lo) * v_s
      at = a_s * e_prev[:, sl]
      rt = r_s * e_pos[:, sl]
      bt = b_s * e_neg[:, sl]
      ktl = kt_s * e_neg[:, sl]
      bc = b_s * e_rest[:, sl]
      kc = kt_s * e_rest[:, sl]
      st = s_sc[d, s]
      ah = _dot_nt(at, st)
      yh = _dot_nt(rt, st)
      u_halves = []
      y_halves = []
      for hf in range(2):
        msk = lo if hf == 0 else jnp.logical_not(lo)
        atm = jnp.where(msk, at, 0.0)
        rtm = jnp.where(msk, rt, 0.0)
        lab = jnp.where(before, _dot_nt(atm, bt), 0.0)
        lak = jnp.where(before, _dot_nt(atm, ktl), 0.0)
        prb = jnp.where(upto, _dot_nt(rtm, bt), 0.0)
        prk = jnp.where(upto, _dot_nt(rtm, ktl), 0.0)
        tinv = eye + lab
        lp = lab
        for _ in range(5):
          lp = _dot(lp, lp)
          tinv = tinv + _dot(tinv, lp)
        u_h = _dot(tinv, ah + _dot(lak, v_s))
        u_halves.append(u_h)
        y_halves.append((prb, prk))
      u = jnp.where(lo, u_halves[0], u_halves[1])
      y_lo = _dot(y_halves[0][0], u) + _dot(y_halves[0][1], v_s)
      y_hi = _dot(y_halves[1][0], u) + _dot(y_halves[1][1], v_s)
      y = yh + jnp.where(lo, y_lo, y_hi)
      upd = _dot_tn(u, bc) + _dot_tn(v_s, kc)
      s_sc[d, s] = st * g_all[:, sl] + jnp.where(blockdiag, upd, 0.0)
      y_ref[:, sl] = _head_norm_slab(y, lo) * lng[0:1, sl] + lnb[0:1, sl] + bonus


def _rwkv(dd, mu, w0, w2p, a0, a2p, rho, kk, ka, lng, lnb, dims):
  b, n, ct = dims
  c = RWKV_CHUNK
  rows = dd.shape[0]
  ncx = ct // c
  ncl = n // c
  base = (b * n) // c
  zw = 768 + LANES

  def fwd(bi, j):
    return (jnp.where(j < ncx, base + bi * ncx + j, bi * ncl + (j - ncx)), 0)

  def bwd(bi, j):
    return (jnp.where(j < ncx, base + bi * ncx + (ncx - 1 - j), bi * ncl + (ncl - 1 - (j - ncx))), 0)

  def const2(bi, j):
    return (0, 0)

  def const3(bi, j):
    return (0, 0, 0)

  return pl.pallas_call(
      functools.partial(_rwkv_kernel, ncx=ncx, ncl=ncl),
      grid=(b, ncx + ncl),
      in_specs=[
          pl.BlockSpec((c, 1152), fwd),
          pl.BlockSpec((c, 1152), bwd),
          pl.BlockSpec((2, zw), const2),
          pl.BlockSpec((2, GROUP_W), const2),
          pl.BlockSpec((2, LANES, GROUP_W), const3),
          pl.BlockSpec((2, GROUP_W), const2),
          pl.BlockSpec((2, LANES, GROUP_W), const3),
          pl.BlockSpec((2, GROUP_W), const2),
          pl.BlockSpec((1, GROUP_W), const2),
          pl.BlockSpec((1, GROUP_W), const2),
          pl.BlockSpec((1, GROUP_W), const2),
          pl.BlockSpec((1, GROUP_W), const2),
      ],
      out_specs=[pl.BlockSpec((c, GROUP_W), fwd), pl.BlockSpec((c, GROUP_W), bwd)],
      out_shape=[jax.ShapeDtypeStruct((rows, GROUP_W), F32)] * 2,
      scratch_shapes=[pltpu.VMEM((2, 2, LANES, LANES), F32), pltpu.VMEM((2, c + 16, zw), F32)],
      compiler_params=pltpu.CompilerParams(
          dimension_semantics=("arbitrary", "arbitrary"), vmem_limit_bytes=VMEM_LIMIT),
      name="rwkv7",
  )(dd, dd, mu, w0, w2p, a0, a2p, rho, kk, ka, lng, lnb)


_Q_HEAD_ORDER = (0, 2, 1, 3)


def _q_perm():
  return np.concatenate([np.arange(h * HEAD_DIM, (h + 1) * HEAD_DIM) for h in _Q_HEAD_ORDER])


def _proj_col_perm():
  qp = _q_perm()
  cols = [qp, np.arange(256, 512), 512 + qp, np.arange(768, 1024), np.arange(1024, 2304)]
  d0 = 2304
  cols.append(np.arange(d0, d0 + 768 + 128))
  wdf, wdb, adf, adb = (np.arange(d0 + 896 + 64 * t, d0 + 896 + 64 * (t + 1)) for t in range(4))
  cols += [wdf, adf, wdb, adb]
  return np.concatenate(cols)


def _rope_tables(n, tail):
  def cos_sin(pos, dim):
    inv = 1.0 / (ROPE_BASE ** (jnp.arange(0, dim, 2, dtype=F32) / dim))
    ang = pos.astype(F32)[:, None] * inv[None, :]
    return jnp.cos(ang), jnp.sin(ang)

  rows = n // GRID_W
  row = jnp.repeat(jnp.arange(rows), GRID_W)
  col = jnp.arange(rows * GRID_W) % GRID_W
  cr, sr = cos_sin(row, HEAD_DIM // 2)
  cc, sc = cos_sin(col, HEAD_DIM // 2)
  cq, sq = cos_sin(jnp.arange(n), HEAD_DIM)
  cax = jnp.concatenate([cr, cr, cc, cc] * 2, axis=-1)
  sax = jnp.concatenate([-sr, sr, -sc, sc] * 2, axis=-1)
  csq = jnp.concatenate([cq, cq] * 2, axis=-1)
  ssq = jnp.concatenate([-sq, sq] * 2, axis=-1)
  ones = jnp.ones((tail, LANES), F32)
  zeros = jnp.zeros((tail, LANES), F32)
  return (jnp.concatenate([cax, ones]), jnp.concatenate([sax, zeros]),
          jnp.concatenate([csq, ones]), jnp.concatenate([ssq, zeros]))


def _retention_tables():
  c = RET_CHUNK
  lg = jnp.log1p(-jnp.exp2(-5.0 - jnp.arange(4, dtype=F32)))
  idx = jnp.arange(c, dtype=F32)
  rel = idx[:, None] - idx[None, :]
  dfw = jnp.where(rel[None] >= 0, jnp.exp(jnp.maximum(rel, 0.0)[None] * lg[:, None, None]), 0.0)
  dec = jnp.stack([dfw, jnp.swapaxes(dfw, 1, 2)])
  lane_lg = jnp.repeat(lg, HEAD_DIM)[None, :]
  qw_f = jnp.exp((idx + 1.0)[:, None] * lane_lg)
  kw_f = jnp.exp((c - 1.0 - idx)[:, None] * lane_lg)
  qw_b = jnp.exp((c - idx)[:, None] * lane_lg)
  kw_b = jnp.exp(idx[:, None] * lane_lg)
  cd = jnp.broadcast_to(jnp.exp(c * lane_lg), (c, GROUP_W))
  lw = jnp.stack([jnp.stack([qw_f, kw_f, cd]), jnp.stack([qw_b, kw_b, cd])])
  return dec, lw


def kernel(x, c, ctx, c_ctx, w_mod, b_mod, norm_g, ffn_w_in, ffn_w_out, w_in, w_out, attn_sink,
           qk_norm_g, ret_norm_g, rwkv_mu, rwkv_w0, rwkv_w2, rwkv_a0, rwkv_a2, rwkv_rho, rwkv_k_k,
           rwkv_k_a, rwkv_g2, rwkv_ln_g, rwkv_ln_b, final_norm_g):
  b, n, d = x.shape
  ct = ctx.shape[1]
  depth = w_mod.shape[0]
  assert d == D_MODEL and b < MOD_ROWS
  assert n % ROW_TILE == 0 and (b * ct) % ROW_TILE == 0 and n % KEY_CHUNK == 0
  assert n % RET_CHUNK == 0 and ct % RET_CHUNK == 0 and n % GRID_W == 0 and n % ct == 0
  dims = (b, n, ct)

  cs = jnp.zeros((MOD_ROWS, d), F32).at[:b].set(c).at[b].set(c_ctx)
  mod = _modulation(cs, w_mod, b_mod).reshape(depth, MOD_ROWS, N_MOD, d)

  tabs = _rope_tables(n, ROW_TILE)
  dec, lw = _retention_tables()
  col_perm = _proj_col_perm()
  q_perm = _q_perm()
  out_perm = np.concatenate([q_perm, 256 + q_perm, np.arange(512, 1024)])
  zpad = jnp.zeros((2, A_RANK, GROUP_W), F32)

  xs = jnp.concatenate([x.reshape(b * n, d), ctx.reshape(b * ct, d)], axis=0)
  for l in range(depth):
    last = l == depth - 1
    w1a, w2a = ffn_w_in[l, 0].astype(BF16), ffn_w_out[l, 0].astype(BF16)
    w1b, w2b = ffn_w_in[l, 1].astype(BF16), ffn_w_out[l, 1].astype(BF16)
    win = w_in[l][:, col_perm].astype(BF16)
    wo = w_out[l][out_perm, :].astype(BF16)
    qkg = jnp.tile(qk_norm_g[l], (1, GROUP_W // HEAD_DIM))
    sink_tab = jnp.broadcast_to(attn_sink[l][jnp.asarray(_Q_HEAD_ORDER)][:, None], (4, LANES))
    w2p = jnp.concatenate([rwkv_w2[l], zpad], axis=1)
    a2p = jnp.concatenate([zpad, rwkv_a2[l]], axis=1)

    xs, qa, kva, qb, kvb, rc, gc, dd = _ffn_proj(
        xs, mod[l], norm_g[l], w1a, w2a, win, qkg, tabs, dims)
    oa = _win_attn(qa, kva, sink_tab, dims)
    ob = _glob_attn(qb, kvb, dims, latent=True)
    if not last:
      ob = jnp.concatenate([ob, _glob_attn(qb, kvb, dims, latent=False)], axis=0)
    cf, cb = _retention(rc, gc, dec, lw, ret_norm_g[l][None, :], dims)
    yf, yb = _rwkv(dd, rwkv_mu[l], rwkv_w0[l], w2p, rwkv_a0[l], a2p,
                   rwkv_rho[l].reshape(2, GROUP_W), rwkv_k_k[l][None, :], rwkv_k_a[l][None, :],
                   rwkv_ln_g[l][None, :], rwkv_ln_b[l][None, :], dims)
    xs = _out_ffn(xs, mod[l], norm_g[l], oa, ob, cf, cb, yf, yb, dd, rwkv_g2[l].astype(BF16), wo,
                  w1b, w2b, final_norm_g[None, :], dims, final=last)
  return xs.reshape(b, n, d)
```

```python
import functools

import numpy as np
import jax
import jax.numpy as jnp
from jax import lax
from jax.experimental import pallas as pl
from jax.experimental.pallas import tpu as pltpu

F32 = jnp.float32
BF16 = jnp.bfloat16

D_MODEL = 1024
HEAD_DIM = 64
N_HEADS = 4
GROUP_W = 256
KV_W = 128
GRID_W = 64
BLOCK = 128
D_FF = 2816
A_RANK = 64
G_RANK = 128
N_MOD = 9
ROPE_BASE = 10000.0
RMS_EPS = 1e-6
GN_EPS = 64e-5
DECAY_SCALE = 0.6065306597126334
PROJ_COLS = 3456
D_COLS = 1152
LANES = 128
MOD_ROWS = 8

ROW_TILE = 256
FF_CHUNK = 1408
Q_TILE = 256
KEY_CHUNK = 1024
RET_CHUNK = 256
RWKV_CHUNK = 32
VMEM_LIMIT = 56 * 1024 * 1024

NEG_BIG = -1e30
HIGHEST = lax.Precision.HIGHEST


def _dot(a, b):
  return jnp.dot(a.astype(BF16), b.astype(BF16), preferred_element_type=F32)


def _dot_nt(a, b):
  return lax.dot_general(a.astype(BF16), b.astype(BF16), (((1,), (1,)), ((), ())),
                         preferred_element_type=F32)


def _dot_tn(a, b):
  return jnp.dot(a.astype(F32).T.astype(BF16), b.astype(BF16), preferred_element_type=F32)


def _silu(x):
  return x * jax.nn.sigmoid(x)


def _lane_lo(shape):
  return (lax.broadcasted_iota(jnp.int32, shape, len(shape) - 1) % LANES) < HEAD_DIM


def _half_sum(x, lo):
  s_lo = jnp.sum(jnp.where(lo, x, 0.0), axis=-1, keepdims=True)
  s_all = jnp.sum(x, axis=-1, keepdims=True)
  return jnp.where(lo, s_lo, s_all - s_lo)


def _head_sum(x):
  lo = _lane_lo((1, LANES))
  return jnp.concatenate([_half_sum(x[:, o:o + LANES], lo) for o in range(0, GROUP_W, LANES)], axis=-1)


def _head_norm_slab(y, lo):
  mu = _half_sum(y, lo) * (1.0 / HEAD_DIM)
  d = y - mu
  var = _half_sum(d * d, lo) * (1.0 / HEAD_DIM)
  return d * lax.rsqrt(var + GN_EPS)


def _rope_slab(x, c, s, half):
  first = (lax.broadcasted_iota(jnp.int32, x.shape, 1) % (2 * half)) < half
  partner = jnp.where(first, pltpu.roll(x, LANES - half, axis=1), pltpu.roll(x, half, axis=1))
  return x * c + partner * s


def _mod_kernel(c_ref, w_ref, b_ref, o_ref):
  s = _silu(c_ref[...])
  o_ref[...] = jnp.dot(s, w_ref[...], precision=HIGHEST, preferred_element_type=F32) + b_ref[...]


def _modulation(cs, w_mod, b_mod):
  depth = w_mod.shape[0]
  return pl.pallas_call(
      _mod_kernel,
      grid=(depth, N_MOD),
      in_specs=[
          pl.BlockSpec((MOD_ROWS, D_MODEL), lambda l, j: (0, 0)),
          pl.BlockSpec((None, D_MODEL, D_MODEL), lambda l, j: (l, 0, j)),
          pl.BlockSpec((None, 1, D_MODEL), lambda l, j: (l, 0, j)),
      ],
      out_specs=pl.BlockSpec((None, MOD_ROWS, D_MODEL), lambda l, j: (l, 0, j)),
      out_shape=jax.ShapeDtypeStruct((depth, MOD_ROWS, N_MOD * D_MODEL), F32),
      compiler_params=pltpu.CompilerParams(
          dimension_semantics=("arbitrary", "arbitrary"), vmem_limit_bytes=VMEM_LIMIT),
      name="modulation",
  )(cs, w_mod, b_mod.reshape(depth, 1, N_MOD * D_MODEL))


def _rms_mod(x, g, shift, scale):
  h = x * lax.rsqrt(jnp.mean(x * x, axis=-1, keepdims=True) + RMS_EPS) * g
  return h * (1.0 + scale) + shift


def _swiglu(h, w1_ref, w2_ref):
  hb = h.astype(BF16)
  acc = None
  for c0 in range(0, D_FF, FF_CHUNK):
    u1 = jnp.dot(hb, w1_ref[:, c0:c0 + FF_CHUNK], preferred_element_type=F32)
    u2 = jnp.dot(hb, w1_ref[:, D_FF + c0:D_FF + c0 + FF_CHUNK], preferred_element_type=F32)
    a = (_silu(u1) * u2).astype(BF16)
    part = jnp.dot(a, w2_ref[c0:c0 + FF_CHUNK, :], preferred_element_type=F32)
    acc = part if acc is None else acc + part
  return acc


def _head_rms(x, g):
  w = x.shape[-1]
  r = lax.broadcasted_iota(jnp.int32, (w, w), 0) // HEAD_DIM
  c = lax.broadcasted_iota(jnp.int32, (w, w), 1) // HEAD_DIM
  avg = jnp.where(r == c, 1.0 / HEAD_DIM, 0.0)
  ms = jnp.dot(x * x, avg, precision=HIGHEST, preferred_element_type=F32)
  return x * lax.rsqrt(ms + RMS_EPS) * g


def _rope_wide(x, c, s, half):
  parts = [_rope_slab(x[:, o:o + LANES], c, s, half) for o in range(0, x.shape[-1], LANES)]
  return parts[0] if len(parts) == 1 else jnp.concatenate(parts, axis=-1)


def _ffn_proj_kernel(x_ref, mod_ref, ng_ref, w1_ref, w2_ref, win_ref, qkg_ref,
                     cax_ref, sax_ref, csq_ref, ssq_ref,
                     xo_ref, qa_ref, kva_ref, qb_ref, kvb_ref, rc_ref, gc_ref, dd_ref):
  x = x_ref[...]
  m = mod_ref[...]
  ng = ng_ref[...]
  h = _rms_mod(x, ng[0:1], m[0:1], m[1:2])
  x1 = x + 0.5 * m[2:3] * _swiglu(h, w1_ref, w2_ref)
  xo_ref[...] = x1
  hb = _rms_mod(x1, ng[1:2], m[3:4], m[4:5]).astype(BF16)

  cax, sax = cax_ref[...], sax_ref[...]
  csq, ssq = csq_ref[...], ssq_ref[...]
  qkg = qkg_ref[...]
  scale = HEAD_DIM ** -0.5

  p = jnp.dot(hb, win_ref[:, 0:512], preferred_element_type=F32)
  qa_ref[...] = (_rope_wide(p[:, 0:256], cax, sax, 16) * scale).astype(BF16)
  kva_ref[:, 0:128] = _rope_wide(p[:, 256:384], cax, sax, 16).astype(BF16)
  kva_ref[:, 128:256] = p[:, 384:512].astype(BF16)
  p = jnp.dot(hb, win_ref[:, 512:1024], preferred_element_type=F32)
  qn = _head_rms(p[:, 0:256], qkg[0:1, :])
  kn = _head_rms(p[:, 256:384], qkg[1:2, 0:128])
  qb_ref[...] = (_rope_wide(qn, cax, sax, 16) * scale).astype(BF16)
  kvb_ref[:, 0:128] = _rope_wide(kn, cax, sax, 16).astype(BF16)
  kvb_ref[:, 128:256] = p[:, 384:512].astype(BF16)
  p = jnp.dot(hb, win_ref[:, 1024:2304], preferred_element_type=F32)
  rc_ref[:, 0:256] = _rope_wide(p[:, 0:256], csq, ssq, 32).astype(BF16)
  rc_ref[:, 256:512] = (_rope_wide(p[:, 256:512], csq, ssq, 32) * scale).astype(BF16)
  rc_ref[:, 512:768] = p[:, 512:768].astype(BF16)
  gc_ref[...] = p[:, 768:1280]
  dd_ref[...] = jnp.dot(hb, win_ref[:, 2304:3456], preferred_element_type=F32)


def _ffn_proj(x, mod, ng, w1, w2, win, qkg, tabs, dims):
  b, n, ct = dims
  rows = x.shape[0]
  tm = ROW_TILE
  tpb = (n + ct) // tm
  ctt = ct // tm
  ident = n // tm

  def row(i):
    return (i, 0)

  def const(i):
    return (0, 0)

  def mod_idx(i):
    return (jnp.where(i % tpb < ctt, b, i // tpb), 0, 0)

  def tab_idx(i):
    t = i % tpb
    return (jnp.where(t < ctt, ident, t - ctt), 0)

  widths = (D_MODEL, 256, 256, 256, 256, 768, 512, D_COLS)
  dtypes = (F32, BF16, BF16, BF16, BF16, BF16, F32, F32)
  return pl.pallas_call(
      _ffn_proj_kernel,
      grid=(rows // tm,),
      in_specs=[
          pl.BlockSpec((tm, D_MODEL), row),
          pl.BlockSpec((None, N_MOD, D_MODEL), mod_idx),
          pl.BlockSpec((3, D_MODEL), const),
          pl.BlockSpec((D_MODEL, 2 * D_FF), const),
          pl.BlockSpec((D_FF, D_MODEL), const),
          pl.BlockSpec((D_MODEL, PROJ_COLS), const),
          pl.BlockSpec((2, GROUP_W), const),
          pl.BlockSpec((tm, LANES), tab_idx),
          pl.BlockSpec((tm, LANES), tab_idx),
          pl.BlockSpec((tm, LANES), tab_idx),
          pl.BlockSpec((tm, LANES), tab_idx),
      ],
      out_specs=[pl.BlockSpec((tm, w), row) for w in widths],
      out_shape=[jax.ShapeDtypeStruct((rows, w), dt) for w, dt in zip(widths, dtypes)],
      compiler_params=pltpu.CompilerParams(
          dimension_semantics=("arbitrary",), vmem_limit_bytes=VMEM_LIMIT),
      name="ffn_proj",
  )(x, mod, ng, w1, w2, win, qkg, *tabs)


def _out_ffn_kernel(x_ref, mod_ref, ng_ref, oa_ref, ob_ref, cf_ref, cb_ref, yf_ref, yb_ref,
                    gd_ref, g2_ref, wo_ref, w1_ref, w2_ref, fg_ref, xo_ref, *, final):
  x = x_ref[...]
  m = mod_ref[...]
  ng = ng_ref[...]
  gate = _dot(jax.nn.sigmoid(gd_ref[...]), g2_ref[...])
  od = (yf_ref[...] + yb_ref[...]) * gate
  oc = cf_ref[...] + cb_ref[...]
  y = jnp.dot(oa_ref[...], wo_ref[0:256, :], preferred_element_type=F32)
  y = y + jnp.dot(ob_ref[...], wo_ref[256:512, :], preferred_element_type=F32)
  y = y + jnp.dot(oc.astype(BF16), wo_ref[512:768, :], preferred_element_type=F32)
  y = y + jnp.dot(od.astype(BF16), wo_ref[768:1024, :], preferred_element_type=F32)
  x1 = x + m[5:6] * y
  h = _rms_mod(x1, ng[2:3], m[6:7], m[7:8])
  x2 = x1 + 0.5 * m[8:9] * _swiglu(h, w1_ref, w2_ref)
  if final:
    x2 = x2 * lax.rsqrt(jnp.mean(x2 * x2, axis=-1, keepdims=True) + RMS_EPS) * fg_ref[...]
  xo_ref[...] = x2


def _out_ffn(x, mod, ng, oa, ob, cf, cb, yf, yb, dd, g2, wo, w1, w2, fg, dims, final):
  b, n, ct = dims
  tm = ROW_TILE
  tpb = (n + ct) // tm
  ctt = ct // tm
  if final:
    grid = (b, n // tm)
    row = lambda bi, j: (bi * tpb + ctt + j, 0)
    out_row = lambda bi, j: (bi * (n // tm) + j, 0)
    mod_idx = lambda bi, j: (bi, 0, 0)
    gd_idx = lambda bi, j: (bi * tpb + ctt + j, 768 // G_RANK)
    out_rows = b * n
  else:
    grid = (b, tpb)
    row = lambda bi, j: (bi * tpb + j, 0)
    out_row = row
    mod_idx = lambda bi, j: (jnp.where(j < ctt, b, bi), 0, 0)
    gd_idx = lambda bi, j: (bi * tpb + j, 768 // G_RANK)
    out_rows = x.shape[0]
  const = lambda bi, j: (0, 0)

  return pl.pallas_call(
      functools.partial(_out_ffn_kernel, final=final),
      grid=grid,
      in_specs=[
          pl.BlockSpec((tm, D_MODEL), row),
          pl.BlockSpec((None, N_MOD, D_MODEL), mod_idx),
          pl.BlockSpec((3, D_MODEL), const),
          pl.BlockSpec((tm, GROUP_W), row),
          pl.BlockSpec((tm, GROUP_W), row),
          pl.BlockSpec((tm, GROUP_W), row),
          pl.BlockSpec((tm, GROUP_W), row),
          pl.BlockSpec((tm, GROUP_W), row),
          pl.BlockSpec((tm, GROUP_W), row),
          pl.BlockSpec((tm, G_RANK), gd_idx),
          pl.BlockSpec((G_RANK, GROUP_W), const),
          pl.BlockSpec((D_MODEL, D_MODEL), const),
          pl.BlockSpec((D_MODEL, 2 * D_FF), const),
          pl.BlockSpec((D_FF, D_MODEL), const),
          pl.BlockSpec((1, D_MODEL), const),
      ],
      out_specs=pl.BlockSpec((tm, D_MODEL), out_row),
      out_shape=jax.ShapeDtypeStruct((out_rows, D_MODEL), F32),
      compiler_params=pltpu.CompilerParams(
          dimension_semantics=("arbitrary", "arbitrary"), vmem_limit_bytes=VMEM_LIMIT),
      name="out_ffn_final" if final else "out_ffn",
  )(x, mod, ng, oa, ob, cf, cb, yf, yb, dd, g2, wo, w1, w2, fg)


def _softmax_heads(q, segs, sink_ref, o_ref):
  lo = _lane_lo((1, LANES))
  for s in range(2):
    qs = q[:, s * LANES:(s + 1) * LANES]
    halves = []
    for hf in range(2):
      qm = jnp.where(lo if hf == 0 else jnp.logical_not(lo), qs, jnp.zeros_like(qs))
      scores = []
      for kv, valid in segs:
        sc = _dot_nt(qm, kv[:, 0:LANES])
        if valid is not None:
          sc = jnp.where(valid, sc, NEG_BIG)
        scores.append(sc)
      mx = scores[0].max(axis=-1, keepdims=True)
      for sc in scores[1:]:
        mx = jnp.maximum(mx, sc.max(axis=-1, keepdims=True))
      snk = sink_ref[2 * s + hf:2 * s + hf + 1, 0:1]
      mx = jnp.maximum(mx, snk)
      den = jnp.exp(snk - mx)
      acc = None
      for sc, (kv, _) in zip(scores, segs):
        p = jnp.exp(sc - mx)
        den = den + p.sum(axis=-1, keepdims=True)
        pv = jnp.dot(p.astype(BF16), kv[:, LANES:2 * LANES], preferred_element_type=F32)
        acc = pv if acc is None else acc + pv
      halves.append(acc / den)
    o_ref[:, s * LANES:(s + 1) * LANES] = jnp.where(lo, halves[0], halves[1]).astype(BF16)


def _win_attn_kernel(q_ref, kp_ref, kc_ref, kn_ref, kx_ref, sink_ref, o_ref, *, bpb, cb, nb):
  t = pl.program_id(0) % bpb
  n = t - cb
  is_lat = t >= cb
  big = 4 * BLOCK
  thr_prev = jnp.where(jnp.logical_and(is_lat, n > 0), 0, big)
  thr_cur = jnp.where(is_lat, -big, big)
  thr_next = jnp.where(jnp.logical_and(is_lat, n < nb - 1), 0, big)
  r = lax.broadcasted_iota(jnp.int32, (BLOCK, BLOCK), 0)
  c = lax.broadcasted_iota(jnp.int32, (BLOCK, BLOCK), 1)
  segs = [
      (kp_ref[...], (c - r) >= thr_prev),
      (kc_ref[...], (c - r) >= thr_cur),
      (kn_ref[...], (r - c) >= thr_next),
      (kx_ref[...], None),
  ]
  _softmax_heads(q_ref[...], segs, sink_ref, o_ref)


def _win_attn(qa, kva, sink_tab, dims):
  b, n, ct = dims
  rows = qa.shape[0]
  nb = n // BLOCK
  cb = ct // BLOCK
  bpb = nb + cb

  def prev(i):
    return (jnp.where(i % bpb > cb, i - 1, i), 0)

  def nxt(i):
    t = i % bpb
    return (jnp.where(jnp.logical_and(t >= cb, t < bpb - 1), i + 1, i), 0)

  return pl.pallas_call(
      functools.partial(_win_attn_kernel, bpb=bpb, cb=cb, nb=nb),
      grid=(rows // BLOCK,),
      in_specs=[
          pl.BlockSpec((BLOCK, GROUP_W), lambda i: (i, 0)),
          pl.BlockSpec((BLOCK, GROUP_W), prev),
          pl.BlockSpec((BLOCK, GROUP_W), lambda i: (i, 0)),
          pl.BlockSpec((BLOCK, GROUP_W), nxt),
          pl.BlockSpec((ct, GROUP_W), lambda i: ((i // bpb) * ((n + ct) // ct), 0)),
          pl.BlockSpec((4, LANES), lambda i: (0, 0)),
      ],
      out_specs=pl.BlockSpec((BLOCK, GROUP_W), lambda i: (i, 0)),
      out_shape=jax.ShapeDtypeStruct((rows, GROUP_W), BF16),
      compiler_params=pltpu.CompilerParams(
          dimension_semantics=("arbitrary",), vmem_limit_bytes=VMEM_LIMIT),
      name="window_attn",
  )(qa, kva, kva, kva, kva, sink_tab)


def _glob_attn_kernel(q_ref, kv_ref, o_ref, m_sc, l_sc, acc_sc, *, ct, ctq, n_chunks):
  lo = _lane_lo((1, LANES))
  m_sc[...] = jnp.full(m_sc.shape, NEG_BIG, F32)
  l_sc[...] = jnp.zeros(l_sc.shape, F32)
  acc_sc[...] = jnp.zeros(acc_sc.shape, F32)

  def step(kv):
    nk = kv.shape[0]
    k = kv[:, 0:LANES]
    v = kv[:, LANES:2 * LANES]
    for s in range(2):
      qs = q_ref[:, s * LANES:(s + 1) * LANES]
      for hf in range(2):
        h = 2 * s + hf
        qm = jnp.where(lo if hf == 0 else jnp.logical_not(lo), qs, jnp.zeros_like(qs))
        sc = _dot_nt(qm, k)
        m_prev = m_sc[h]
        m_new = jnp.maximum(m_prev, sc.max(axis=-1, keepdims=True))
        alpha = jnp.exp(m_prev - m_new)
        p = jnp.exp(sc - pltpu.repeat(m_new, nk // LANES, axis=1))
        l_sc[h] = alpha * l_sc[h] + p.sum(axis=-1, keepdims=True)
        acc_sc[h] = alpha * acc_sc[h] + jnp.dot(p.astype(BF16), v, preferred_element_type=F32)
        m_sc[h] = m_new

  def body(j, carry):
    start = pl.multiple_of(ct + j * KEY_CHUNK, ct)
    step(kv_ref[pl.ds(start, KEY_CHUNK), :])
    return carry

  trips = jnp.where(pl.program_id(1) < ctq, 0, n_chunks)
  lax.fori_loop(0, trips, body, 0)
  step(kv_ref[0:ct, :])

  for s in range(2):
    o_lo = acc_sc[2 * s] / l_sc[2 * s]
    o_hi = acc_sc[2 * s + 1] / l_sc[2 * s + 1]
    o_ref[:, s * LANES:(s + 1) * LANES] = jnp.where(lo, o_lo, o_hi).astype(BF16)


def _glob_attn(qb, kvb, dims):
  b, n, ct = dims
  tq = Q_TILE
  sa = n + ct
  return pl.pallas_call(
      functools.partial(_glob_attn_kernel, ct=ct, ctq=ct // tq, n_chunks=n // KEY_CHUNK),
      grid=(b, sa // tq),
      in_specs=[
          pl.BlockSpec((tq, GROUP_W), lambda bi, j: (bi * (sa // tq) + j, 0)),
          pl.BlockSpec((sa, GROUP_W), lambda bi, j: (bi, 0)),
      ],
      out_specs=pl.BlockSpec((tq, GROUP_W), lambda bi, j: (bi * (sa // tq) + j, 0)),
      out_shape=jax.ShapeDtypeStruct((b * sa, GROUP_W), BF16),
      scratch_shapes=[pltpu.VMEM((4, tq, LANES), F32) for _ in range(3)],
      compiler_params=pltpu.CompilerParams(
          dimension_semantics=("arbitrary", "arbitrary"), vmem_limit_bytes=VMEM_LIMIT),
      name="global_attn",
  )(qb, kvb)


def _ret_kernel(rf_ref, rb_ref, gf_ref, gb_ref, dec_ref, lw_ref, ng_ref, of_ref, ob_ref,
                sf_sc, sb_sc):
  j = pl.program_id(1)

  @pl.when(j == 0)
  def _():
    sf_sc[...] = jnp.zeros(sf_sc.shape, F32)
    sb_sc[...] = jnp.zeros(sb_sc.shape, F32)

  lo = _lane_lo((1, LANES))
  rr = lax.broadcasted_iota(jnp.int32, (LANES, LANES), 0) // HEAD_DIM
  cc = lax.broadcasted_iota(jnp.int32, (LANES, LANES), 1) // HEAD_DIM
  blockdiag = rr == cc
  ng = ng_ref[...]

  for d, (r_ref, g_ref, o_ref, s_sc) in enumerate(
      ((rf_ref, gf_ref, of_ref, sf_sc), (rb_ref, gb_ref, ob_ref, sb_sc))):
    for s in range(2):
      sl = slice(s * LANES, (s + 1) * LANES)
      q = r_ref[:, s * LANES:(s + 1) * LANES]
      k = r_ref[:, 256 + s * LANES:256 + (s + 1) * LANES]
      v = r_ref[:, 512 + s * LANES:512 + (s + 1) * LANES]
      qw = lw_ref[d, 0][:, sl]
      kw = lw_ref[d, 1][:, sl]
      cd = lw_ref[d, 2][0:1, sl]
      st = s_sc[s]
      o = _dot(q.astype(F32) * qw, st)
      halves = []
      for hf in range(2):
        qm = jnp.where(lo if hf == 0 else jnp.logical_not(lo), q, jnp.zeros_like(q))
        att = _dot_nt(qm, k) * dec_ref[d, 2 * s + hf]
        halves.append(_dot(att, v))
      o = o + jnp.where(lo, halves[0], halves[1])
      u = _dot_tn(k.astype(F32) * kw, v)
      s_sc[s] = st * cd + jnp.where(blockdiag, u, 0.0)
      gate = g_ref[:, d * GROUP_W + s * LANES:d * GROUP_W + (s + 1) * LANES]
      o_ref[:, sl] = _head_norm_slab(o, lo) * ng[0:1, sl] * _silu(gate)


def _scan_maps(bpb, ncx):
  def fwd(j):
    return j

  def bwd(j):
    return jnp.where(j < ncx, ncx - 1 - j, bpb - 1 - (j - ncx))

  return fwd, bwd


def _retention(rc, gc, dec, lw, ng, dims):
  b, n, ct = dims
  c = RET_CHUNK
  rows = rc.shape[0]
  bpb = (n + ct) // c
  f, r = _scan_maps(bpb, ct // c)
  fwd = lambda bi, j: (bi * bpb + f(j), 0)
  bwd = lambda bi, j: (bi * bpb + r(j), 0)

  return pl.pallas_call(
      _ret_kernel,
      grid=(b, bpb),
      in_specs=[
          pl.BlockSpec((c, 768), fwd),
          pl.BlockSpec((c, 768), bwd),
          pl.BlockSpec((c, 512), fwd),
          pl.BlockSpec((c, 512), bwd),
          pl.BlockSpec((2, 4, c, c), lambda bi, j: (0, 0, 0, 0)),
          pl.BlockSpec((2, 3, c, GROUP_W), lambda bi, j: (0, 0, 0, 0)),
          pl.BlockSpec((1, GROUP_W), lambda bi, j: (0, 0)),
      ],
      out_specs=[pl.BlockSpec((c, GROUP_W), fwd), pl.BlockSpec((c, GROUP_W), bwd)],
      out_shape=[jax.ShapeDtypeStruct((rows, GROUP_W), F32)] * 2,
      scratch_shapes=[pltpu.VMEM((2, LANES, LANES), F32)] * 2,
      compiler_params=pltpu.CompilerParams(
          dimension_semantics=("arbitrary", "arbitrary"), vmem_limit_bytes=VMEM_LIMIT),
      name="retention",
  )(rc, rc, gc, gc, dec, lw, ng)


def _rwkv_kernel(df_ref, db_ref, mu_ref, w0_ref, w2_ref, a0_ref, a2_ref, rho_ref, kk_ref, ka_ref,
                 lng_ref, lnb_ref, yf_ref, yb_ref, s_sc, z_sc, *, ncx, nbatch):
  j = pl.program_id(0)
  c = RWKV_CHUNK
  sr = N_HEADS * c
  at_start = jnp.logical_or(j == 0, j == ncx)

  @pl.when(j == 0)
  def _():
    s_sc[...] = jnp.zeros(s_sc.shape, F32)

  @pl.when(at_start)
  def _():
    z_sc[:, :, 0:8, :] = jnp.zeros((2, nbatch, 8, z_sc.shape[3]), F32)
    z_sc[:, :, c + 8:c + 16, :] = jnp.zeros((2, nbatch, 8, z_sc.shape[3]), F32)

  lane_head = lax.broadcasted_iota(jnp.int32, (1, GROUP_W), 1) // HEAD_DIM
  head_masks = [lane_head == h for h in range(N_HEADS)]
  rr = lax.broadcasted_iota(jnp.int32, (GROUP_W, GROUP_W), 0) // HEAD_DIM
  cc = lax.broadcasted_iota(jnp.int32, (GROUP_W, GROUP_W), 1) // HEAD_DIM
  blockdiag = rr == cc
  ti = lax.broadcasted_iota(jnp.int32, (c, c), 0)
  si = lax.broadcasted_iota(jnp.int32, (c, c), 1)
  ts = lax.broadcasted_iota(jnp.int32, (sr, sr), 0)
  ss = lax.broadcasted_iota(jnp.int32, (sr, sr), 1)
  eye = jnp.where(ts == ss, 1.0, 0.0)
  ts, ss = ts % c, ss % c
  kkp = kk_ref[...]
  kap = ka_ref[...]
  lng = lng_ref[...]
  lnb = lnb_ref[...]
  lo = _lane_lo((1, LANES))

  def stack(x):
    return jnp.concatenate([jnp.where(m, x, 0.0) for m in head_masks], axis=0)

  def unstack(x):
    return x[0:c] + x[c:2 * c] + x[2 * c:3 * c] + x[3 * c:4 * c]

  chains = [(d, bi) for d in range(2) for bi in range(nbatch)]
  st = {}

  for ch in chains:
    d, bi = ch
    d_ref = df_ref if d == 0 else db_ref
    upto_c = (si <= ti) if d == 0 else (si >= ti)
    z = jnp.concatenate([d_ref[bi, :, 0:768], d_ref[bi, :, 896 + d * LANES:1024 + d * LANES]], axis=-1)
    z_sc[d, bi, 8:c + 8, :] = z
    zs = z_sc[d, bi, 7:c + 7, :] if d == 0 else z_sc[d, bi, 9:c + 9, :]
    zm = z + mu_ref[d:d + 1, :] * (zs - z)
    if d == 0:
      z_sc[d, bi, 7:8, :] = z[c - 1:c, :]
    else:
      z_sc[d, bi, c + 8:c + 9, :] = z[0:1, :]
    r = zm[:, 0:256]
    k = zm[:, 256:512]
    v = zm[:, 512:768]
    wa = zm[:, 768:896]
    logw = -DECAY_SCALE * jax.nn.sigmoid(w0_ref[d:d + 1, :] + _dot(jnp.tanh(wa), w2_ref[d]))
    ag = jax.nn.sigmoid(a0_ref[d:d + 1, :] + _dot(wa, a2_ref[d]))
    kkr = k * kkp
    kkn = kkr / jnp.maximum(jnp.sqrt(_head_sum(kkr * kkr)), 1e-12)
    kt = k * (1.0 + (ag - 1.0) * kap)
    bonus = _head_sum(r * kt * rho_ref[d:d + 1, :]) * v
    cum = jnp.dot(jnp.where(upto_c, 1.0, 0.0), logw, precision=HIGHEST, preferred_element_type=F32)
    tot = cum[c - 1:c, :] if d == 0 else cum[0:1, :]
    e_neg = jnp.exp(-cum)
    e_rest = jnp.exp(tot - cum)
    bvec = kkn * ag
    at = -kkn * jnp.exp(cum - logw)
    rt = r * jnp.exp(cum)
    st[ch] = dict(
        v=v, bonus=bonus, g_all=jnp.exp(tot), at=at, rt=rt,
        x=jnp.concatenate([stack(at), stack(rt)], axis=0),
        y=jnp.concatenate([stack(bvec * e_neg), stack(kt * e_neg)], axis=0),
        bk=jnp.concatenate([bvec * e_rest, kt * e_rest], axis=0),
        vm=stack(v))

  for ch in chains:
    d, bi = ch
    e = st[ch]
    before = (ss < ts) if d == 0 else (ss > ts)
    upto = (ss <= ts) if d == 0 else (ss >= ts)
    a_all = _dot_nt(e["x"], e["y"])
    e["lab"] = jnp.where(before, a_all[0:sr, 0:sr], 0.0)
    lak = jnp.where(before, a_all[0:sr, sr:2 * sr], 0.0)
    e["p"] = jnp.concatenate([jnp.where(upto, a_all[sr:2 * sr, 0:sr], 0.0),
                              jnp.where(upto, a_all[sr:2 * sr, sr:2 * sr], 0.0)], axis=-1)
    s0 = s_sc[d, bi]
    e["s0"] = s0
    ar = _dot_nt(jnp.concatenate([e["at"], e["rt"]], axis=0), s0)
    e["rh"] = ar[c:2 * c]
    e["xs"] = stack(ar[0:c]) + _dot(lak, e["vm"])

  for ch in chains:
    e = st[ch]
    e["lp"] = _dot(e["lab"], e["lab"])
    e["t"] = eye + e["lab"]
  for it in range(4):
    for ch in chains:
      e = st[ch]
      lp = e["lp"]
      e["t"] = e["t"] + _dot(e["t"], lp)
      if it < 3:
        e["lp"] = _dot(lp, lp)

  for ch in chains:
    e = st[ch]
    e["us"] = _dot(e["t"], e["xs"])
  for ch in chains:
    d, bi = ch
    e = st[ch]
    ys = _dot(e["p"], jnp.concatenate([e["us"], e["vm"]], axis=0))
    y = e["rh"] + unstack(ys)
    u = unstack(e["us"])
    upd = _dot_tn(jnp.concatenate([u, e["v"]], axis=0), e["bk"])
    s_sc[d, bi] = e["s0"] * e["g_all"] + jnp.where(blockdiag, upd, 0.0)
    y_ref = yf_ref if d == 0 else yb_ref
    for s in range(2):
      sl = slice(s * LANES, (s + 1) * LANES)
      y_ref[bi, :, sl] = (_head_norm_slab(y[:, sl], lo) * lng[0:1, sl] + lnb[0:1, sl]
                          + e["bonus"][:, sl])


def _rwkv(dd, mu, w0, w2p, a0, a2p, rho, kk, ka, lng, lnb, dims):
  b, n, ct = dims
  c = RWKV_CHUNK
  sa = n + ct
  bpb = sa // c
  ncx = ct // c
  zw = 768 + LANES
  f, r = _scan_maps(bpb, ncx)
  fwd = lambda j: (0, f(j), 0)
  bwd = lambda j: (0, r(j), 0)
  const2 = lambda j: (0, 0)
  const3 = lambda j: (0, 0, 0)
  d3 = dd.reshape(b, sa, D_COLS)

  yf, yb = pl.pallas_call(
      functools.partial(_rwkv_kernel, ncx=ncx, nbatch=b),
      grid=(bpb,),
      in_specs=[
          pl.BlockSpec((b, c, D_COLS), fwd),
          pl.BlockSpec((b, c, D_COLS), bwd),
          pl.BlockSpec((2, zw), const2),
          pl.BlockSpec((2, GROUP_W), const2),
          pl.BlockSpec((2, LANES, GROUP_W), const3),
          pl.BlockSpec((2, GROUP_W), const2),
          pl.BlockSpec((2, LANES, GROUP_W), const3),
          pl.BlockSpec((2, GROUP_W), const2),
          pl.BlockSpec((1, GROUP_W), const2),
          pl.BlockSpec((1, GROUP_W), const2),
          pl.BlockSpec((1, GROUP_W), const2),
          pl.BlockSpec((1, GROUP_W), const2),
      ],
      out_specs=[pl.BlockSpec((b, c, GROUP_W), fwd), pl.BlockSpec((b, c, GROUP_W), bwd)],
      out_shape=[jax.ShapeDtypeStruct((b, sa, GROUP_W), F32)] * 2,
      scratch_shapes=[pltpu.VMEM((2, b, GROUP_W, GROUP_W), F32), pltpu.VMEM((2, b, c + 16, zw), F32)],
      compiler_params=pltpu.CompilerParams(
          dimension_semantics=("arbitrary",), vmem_limit_bytes=VMEM_LIMIT),
      name="rwkv7",
  )(d3, d3, mu, w0, w2p, a0, a2p, rho, kk, ka, lng, lnb)
  return yf.reshape(b * sa, GROUP_W), yb.reshape(b * sa, GROUP_W)


_Q_HEAD_ORDER = (0, 2, 1, 3)


def _q_perm():
  return np.concatenate([np.arange(h * HEAD_DIM, (h + 1) * HEAD_DIM) for h in _Q_HEAD_ORDER])


def _proj_col_perm():
  qp = _q_perm()
  cols = [qp, np.arange(256, 512), 512 + qp, np.arange(768, 1024), np.arange(1024, 2304)]
  d0 = 2304
  cols.append(np.arange(d0, d0 + 768 + 128))
  wdf, wdb, adf, adb = (np.arange(d0 + 896 + 64 * t, d0 + 896 + 64 * (t + 1)) for t in range(4))
  cols += [wdf, adf, wdb, adb]
  return np.concatenate(cols)


def _rope_tables(n, tail):
  def cos_sin(pos, dim):
    inv = 1.0 / (ROPE_BASE ** (jnp.arange(0, dim, 2, dtype=F32) / dim))
    ang = pos.astype(F32)[:, None] * inv[None, :]
    return jnp.cos(ang), jnp.sin(ang)

  rows = n // GRID_W
  row = jnp.repeat(jnp.arange(rows), GRID_W)
  col = jnp.arange(rows * GRID_W) % GRID_W
  cr, sr = cos_sin(row, HEAD_DIM // 2)
  cc, sc = cos_sin(col, HEAD_DIM // 2)
  cq, sq = cos_sin(jnp.arange(n), HEAD_DIM)
  cax = jnp.concatenate([cr, cr, cc, cc] * 2, axis=-1)
  sax = jnp.concatenate([-sr, sr, -sc, sc] * 2, axis=-1)
  csq = jnp.concatenate([cq, cq] * 2, axis=-1)
  ssq = jnp.concatenate([-sq, sq] * 2, axis=-1)
  ones = jnp.ones((tail, LANES), F32)
  zeros = jnp.zeros((tail, LANES), F32)
  return (jnp.concatenate([cax, ones]), jnp.concatenate([sax, zeros]),
          jnp.concatenate([csq, ones]), jnp.concatenate([ssq, zeros]))


def _retention_tables():
  c = RET_CHUNK
  lg = jnp.log1p(-jnp.exp2(-5.0 - jnp.arange(4, dtype=F32)))
  idx = jnp.arange(c, dtype=F32)
  rel = idx[:, None] - idx[None, :]
  dfw = jnp.where(rel[None] >= 0, jnp.exp(jnp.maximum(rel, 0.0)[None] * lg[:, None, None]), 0.0)
  dec = jnp.stack([dfw, jnp.swapaxes(dfw, 1, 2)])
  lane_lg = jnp.repeat(lg, HEAD_DIM)[None, :]
  qw_f = jnp.exp((idx + 1.0)[:, None] * lane_lg)
  kw_f = jnp.exp((c - 1.0 - idx)[:, None] * lane_lg)
  qw_b = jnp.exp((c - idx)[:, None] * lane_lg)
  kw_b = jnp.exp(idx[:, None] * lane_lg)
  cd = jnp.broadcast_to(jnp.exp(c * lane_lg), (c, GROUP_W))
  lw = jnp.stack([jnp.stack([qw_f, kw_f, cd]), jnp.stack([qw_b, kw_b, cd])])
  return dec, lw


def kernel(x, c, ctx, c_ctx, w_mod, b_mod, norm_g, ffn_w_in, ffn_w_out, w_in, w_out, attn_sink,
           qk_norm_g, ret_norm_g, rwkv_mu, rwkv_w0, rwkv_w2, rwkv_a0, rwkv_a2, rwkv_rho, rwkv_k_k,
           rwkv_k_a, rwkv_g2, rwkv_ln_g, rwkv_ln_b, final_norm_g):
  b, n, d = x.shape
  ct = ctx.shape[1]
  depth = w_mod.shape[0]
  assert d == D_MODEL and b < MOD_ROWS
  assert n % ROW_TILE == 0 and ct % ROW_TILE == 0 and n % KEY_CHUNK == 0 and ct % Q_TILE == 0
  assert n % RET_CHUNK == 0 and ct % RET_CHUNK == 0 and n % GRID_W == 0 and KEY_CHUNK % ct == 0
  dims = (b, n, ct)

  cs = jnp.zeros((MOD_ROWS, d), F32).at[:b].set(c).at[b].set(c_ctx)
  mod = _modulation(cs, w_mod, b_mod).reshape(depth, MOD_ROWS, N_MOD, d)

  tabs = _rope_tables(n, ROW_TILE)
  dec, lw = _retention_tables()
  col_perm = _proj_col_perm()
  q_perm = _q_perm()
  out_perm = np.concatenate([q_perm, 256 + q_perm, np.arange(512, 1024)])
  zpad = jnp.zeros((2, A_RANK, GROUP_W), F32)

  xs = jnp.concatenate([ctx, x], axis=1).reshape(b * (ct + n), d)
  for l in range(depth):
    last = l == depth - 1
    w1a, w2a = ffn_w_in[l, 0].astype(BF16), ffn_w_out[l, 0].astype(BF16)
    w1b, w2b = ffn_w_in[l, 1].astype(BF16), ffn_w_out[l, 1].astype(BF16)
    win = w_in[l][:, col_perm].astype(BF16)
    wo = w_out[l][out_perm, :].astype(BF16)
    qkg = jnp.tile(qk_norm_g[l], (1, GROUP_W // HEAD_DIM))
    sink_tab = jnp.broadcast_to(attn_sink[l][jnp.asarray(_Q_HEAD_ORDER)][:, None], (4, LANES))
    w2p = jnp.concatenate([rwkv_w2[l], zpad], axis=1)
    a2p = jnp.concatenate([zpad, rwkv_a2[l]], axis=1)

    xs, qa, kva, qb, kvb, rc, gc, dd = _ffn_proj(
        xs, mod[l], norm_g[l], w1a, w2a, win, qkg, tabs, dims)
    oa = _win_attn(qa, kva, sink_tab, dims)
    ob = _glob_attn(qb, kvb, dims)
    cf, cb = _retention(rc, gc, dec, lw, ret_norm_g[l][None, :], dims)
    yf, yb = _rwkv(dd, rwkv_mu[l], rwkv_w0[l], w2p, rwkv_a0[l], a2p,
                   rwkv_rho[l].reshape(2, GROUP_W), rwkv_k_k[l][None, :], rwkv_k_a[l][None, :],
                   rwkv_ln_g[l][None, :], rwkv_ln_b[l][None, :], dims)
    xs = _out_ffn(xs, mod[l], norm_g[l], oa, ob, cf, cb, yf, yb, dd, rwkv_g2[l].astype(BF16), wo,
                  w1b, w2b, final_norm_g[None, :], dims, final=last)
  return xs.reshape(b, n, d)
```

```python
import functools

import numpy as np
import jax
import jax.numpy as jnp
from jax import lax
from jax.experimental import pallas as pl
from jax.experimental.pallas import tpu as pltpu

F32 = jnp.float32
BF16 = jnp.bfloat16

D_MODEL = 1024
HEAD_DIM = 64
N_HEADS = 4
GROUP_W = 256
KV_W = 128
GRID_W = 64
BLOCK = 128
D_FF = 2816
A_RANK = 64
G_RANK = 128
N_MOD = 9
ROPE_BASE = 10000.0
RMS_EPS = 1e-6
GN_EPS = 64e-5
DECAY_SCALE = 0.6065306597126334
PROJ_COLS = 3456
D_COLS = 1152
LANES = 128
MOD_ROWS = 8

ROW_TILE = 256
FF_CHUNK = 1408
Q_TILE = 256
KEY_CHUNK = 1024
RET_CHUNK = 256
RWKV_CHUNK = 32
VMEM_LIMIT = 56 * 1024 * 1024

NEG_BIG = -1e30
LOG2E = 1.4426950408889634
BOUND_MARGIN = 1.02
UNDERFLOW_GUARD = 2.0 ** -100
HIGHEST = lax.Precision.HIGHEST


def _dot(a, b):
  return jnp.dot(a.astype(BF16), b.astype(BF16), preferred_element_type=F32)


def _dot_nt(a, b):
  return lax.dot_general(a.astype(BF16), b.astype(BF16), (((1,), (1,)), ((), ())),
                         preferred_element_type=F32)


def _dot_tn(a, b):
  return jnp.dot(a.astype(F32).T.astype(BF16), b.astype(BF16), preferred_element_type=F32)


def _silu(x):
  return x * jax.nn.sigmoid(x)


def _lane_lo(shape):
  return (lax.broadcasted_iota(jnp.int32, shape, len(shape) - 1) % LANES) < HEAD_DIM


def _half_sum(x, lo):
  s_lo = jnp.sum(jnp.where(lo, x, 0.0), axis=-1, keepdims=True)
  s_all = jnp.sum(x, axis=-1, keepdims=True)
  return jnp.where(lo, s_lo, s_all - s_lo)


def _head_sum(x):
  lo = _lane_lo((1, LANES))
  return jnp.concatenate([_half_sum(x[:, o:o + LANES], lo) for o in range(0, GROUP_W, LANES)], axis=-1)


def _head_norm_slab(y, lo):
  mu = _half_sum(y, lo) * (1.0 / HEAD_DIM)
  d = y - mu
  var = _half_sum(d * d, lo) * (1.0 / HEAD_DIM)
  return d * lax.rsqrt(var + GN_EPS)


def _rope_slab(x, c, s, half):
  first = (lax.broadcasted_iota(jnp.int32, x.shape, 1) % (2 * half)) < half
  partner = jnp.where(first, pltpu.roll(x, LANES - half, axis=1), pltpu.roll(x, half, axis=1))
  return x * c + partner * s


def _mod_kernel(c_ref, w_ref, b_ref, o_ref):
  s = _silu(c_ref[...])
  o_ref[...] = jnp.dot(s, w_ref[...], precision=HIGHEST, preferred_element_type=F32) + b_ref[...]


def _modulation(cs, w_mod, b_mod):
  depth = w_mod.shape[0]
  return pl.pallas_call(
      _mod_kernel,
      grid=(depth, N_MOD),
      in_specs=[
          pl.BlockSpec((MOD_ROWS, D_MODEL), lambda l, j: (0, 0)),
          pl.BlockSpec((None, D_MODEL, D_MODEL), lambda l, j: (l, 0, j)),
          pl.BlockSpec((None, 1, D_MODEL), lambda l, j: (l, 0, j)),
      ],
      out_specs=pl.BlockSpec((None, MOD_ROWS, D_MODEL), lambda l, j: (l, 0, j)),
      out_shape=jax.ShapeDtypeStruct((depth, MOD_ROWS, N_MOD * D_MODEL), F32),
      compiler_params=pltpu.CompilerParams(
          dimension_semantics=("arbitrary", "arbitrary"), vmem_limit_bytes=VMEM_LIMIT),
      name="modulation",
  )(cs, w_mod, b_mod.reshape(depth, 1, N_MOD * D_MODEL))


def _rms_mod(x, g, shift, scale):
  h = x * lax.rsqrt(jnp.mean(x * x, axis=-1, keepdims=True) + RMS_EPS) * g
  return h * (1.0 + scale) + shift


def _swiglu(h, w1_ref, w2_ref):
  hb = h.astype(BF16)
  acc = None
  for c0 in range(0, D_FF, FF_CHUNK):
    u1 = jnp.dot(hb, w1_ref[:, c0:c0 + FF_CHUNK], preferred_element_type=F32)
    u2 = jnp.dot(hb, w1_ref[:, D_FF + c0:D_FF + c0 + FF_CHUNK], preferred_element_type=F32)
    a = (_silu(u1) * u2).astype(BF16)
    part = jnp.dot(a, w2_ref[c0:c0 + FF_CHUNK, :], preferred_element_type=F32)
    acc = part if acc is None else acc + part
  return acc


def _head_rms(x, g):
  w = x.shape[-1]
  r = lax.broadcasted_iota(jnp.int32, (w, w), 0) // HEAD_DIM
  c = lax.broadcasted_iota(jnp.int32, (w, w), 1) // HEAD_DIM
  avg = jnp.where(r == c, 1.0 / HEAD_DIM, 0.0)
  ms = jnp.dot(x * x, avg, precision=HIGHEST, preferred_element_type=F32)
  return x * lax.rsqrt(ms + RMS_EPS) * g


def _rope_wide(x, c, s, half):
  parts = [_rope_slab(x[:, o:o + LANES], c, s, half) for o in range(0, x.shape[-1], LANES)]
  return parts[0] if len(parts) == 1 else jnp.concatenate(parts, axis=-1)


def _ffn_proj_kernel(x_ref, mod_ref, ng_ref, w1_ref, w2_ref, win_ref, qkg_ref,
                     cax_ref, sax_ref, csq_ref, ssq_ref,
                     xo_ref, qa_ref, kva_ref, qb_ref, kvb_ref, rc_ref, gc_ref, dd_ref):
  x = x_ref[...]
  m = mod_ref[...]
  ng = ng_ref[...]
  h = _rms_mod(x, ng[0:1], m[0:1], m[1:2])
  x1 = x + 0.5 * m[2:3] * _swiglu(h, w1_ref, w2_ref)
  xo_ref[...] = x1
  hb = _rms_mod(x1, ng[1:2], m[3:4], m[4:5]).astype(BF16)

  cax, sax = cax_ref[...], sax_ref[...]
  csq, ssq = csq_ref[...], ssq_ref[...]
  qkg = qkg_ref[...]
  scale = HEAD_DIM ** -0.5

  p = jnp.dot(hb, win_ref[:, 0:512], preferred_element_type=F32)
  qa_ref[...] = (_rope_wide(p[:, 0:256], cax, sax, 16) * scale).astype(BF16)
  kva_ref[:, 0:128] = _rope_wide(p[:, 256:384], cax, sax, 16).astype(BF16)
  kva_ref[:, 128:256] = p[:, 384:512].astype(BF16)
  p = jnp.dot(hb, win_ref[:, 512:1024], preferred_element_type=F32)
  qn = _head_rms(p[:, 0:256], qkg[0:1, :])
  kn = _head_rms(p[:, 256:384], qkg[1:2, 0:128])
  qb_ref[...] = (_rope_wide(qn, cax, sax, 16) * (scale * LOG2E)).astype(BF16)
  kvb_ref[:, 0:128] = _rope_wide(kn, cax, sax, 16).astype(BF16)
  kvb_ref[:, 128:256] = p[:, 384:512].astype(BF16)
  p = jnp.dot(hb, win_ref[:, 1024:2304], preferred_element_type=F32)
  rc_ref[:, 0:256] = _rope_wide(p[:, 0:256], csq, ssq, 32).astype(BF16)
  rc_ref[:, 256:512] = (_rope_wide(p[:, 256:512], csq, ssq, 32) * scale).astype(BF16)
  rc_ref[:, 512:768] = p[:, 512:768].astype(BF16)
  gc_ref[...] = p[:, 768:1280]
  dd_ref[...] = jnp.dot(hb, win_ref[:, 2304:3456], preferred_element_type=F32)


def _ffn_proj(x, mod, ng, w1, w2, win, qkg, tabs, dims):
  b, n, ct = dims
  rows = x.shape[0]
  tm = ROW_TILE
  tpb = (n + ct) // tm
  ctt = ct // tm
  ident = n // tm

  def row(i):
    return (i, 0)

  def const(i):
    return (0, 0)

  def mod_idx(i):
    return (jnp.where(i % tpb < ctt, b, i // tpb), 0, 0)

  def tab_idx(i):
    t = i % tpb
    return (jnp.where(t < ctt, ident, t - ctt), 0)

  widths = (D_MODEL, 256, 256, 256, 256, 768, 512, D_COLS)
  dtypes = (F32, BF16, BF16, BF16, BF16, BF16, F32, F32)
  return pl.pallas_call(
      _ffn_proj_kernel,
      grid=(rows // tm,),
      in_specs=[
          pl.BlockSpec((tm, D_MODEL), row),
          pl.BlockSpec((None, N_MOD, D_MODEL), mod_idx),
          pl.BlockSpec((3, D_MODEL), const),
          pl.BlockSpec((D_MODEL, 2 * D_FF), const),
          pl.BlockSpec((D_FF, D_MODEL), const),
          pl.BlockSpec((D_MODEL, PROJ_COLS), const),
          pl.BlockSpec((2, GROUP_W), const),
          pl.BlockSpec((tm, LANES), tab_idx),
          pl.BlockSpec((tm, LANES), tab_idx),
          pl.BlockSpec((tm, LANES), tab_idx),
          pl.BlockSpec((tm, LANES), tab_idx),
      ],
      out_specs=[pl.BlockSpec((tm, w), row) for w in widths],
      out_shape=[jax.ShapeDtypeStruct((rows, w), dt) for w, dt in zip(widths, dtypes)],
      compiler_params=pltpu.CompilerParams(
          dimension_semantics=("arbitrary",), vmem_limit_bytes=VMEM_LIMIT),
      name="ffn_proj",
  )(x, mod, ng, w1, w2, win, qkg, *tabs)


def _out_ffn_kernel(x_ref, mod_ref, ng_ref, oa_ref, ob_ref, cf_ref, cb_ref, yf_ref, yb_ref,
                    gd_ref, g2_ref, wo_ref, w1_ref, w2_ref, fg_ref, xo_ref, *, final):
  x = x_ref[...]
  m = mod_ref[...]
  ng = ng_ref[...]
  gate = _dot(jax.nn.sigmoid(gd_ref[...]), g2_ref[...])
  od = (yf_ref[...] + yb_ref[...]) * gate
  oc = cf_ref[...] + cb_ref[...]
  y = jnp.dot(oa_ref[...], wo_ref[0:256, :], preferred_element_type=F32)
  y = y + jnp.dot(ob_ref[...], wo_ref[256:512, :], preferred_element_type=F32)
  y = y + jnp.dot(oc.astype(BF16), wo_ref[512:768, :], preferred_element_type=F32)
  y = y + jnp.dot(od.astype(BF16), wo_ref[768:1024, :], preferred_element_type=F32)
  x1 = x + m[5:6] * y
  h = _rms_mod(x1, ng[2:3], m[6:7], m[7:8])
  x2 = x1 + 0.5 * m[8:9] * _swiglu(h, w1_ref, w2_ref)
  if final:
    x2 = x2 * lax.rsqrt(jnp.mean(x2 * x2, axis=-1, keepdims=True) + RMS_EPS) * fg_ref[...]
  xo_ref[...] = x2


def _out_ffn(x, mod, ng, oa, ob, cf, cb, yf, yb, dd, g2, wo, w1, w2, fg, dims, final):
  b, n, ct = dims
  tm = ROW_TILE
  tpb = (n + ct) // tm
  ctt = ct // tm
  if final:
    grid = (b, n // tm)
    row = lambda bi, j: (bi * tpb + ctt + j, 0)
    out_row = lambda bi, j: (bi * (n // tm) + j, 0)
    mod_idx = lambda bi, j: (bi, 0, 0)
    gd_idx = lambda bi, j: (bi * tpb + ctt + j, 768 // G_RANK)
    out_rows = b * n
  else:
    grid = (b, tpb)
    row = lambda bi, j: (bi * tpb + j, 0)
    out_row = row
    mod_idx = lambda bi, j: (jnp.where(j < ctt, b, bi), 0, 0)
    gd_idx = lambda bi, j: (bi * tpb + j, 768 // G_RANK)
    out_rows = x.shape[0]
  const = lambda bi, j: (0, 0)

  return pl.pallas_call(
      functools.partial(_out_ffn_kernel, final=final),
      grid=grid,
      in_specs=[
          pl.BlockSpec((tm, D_MODEL), row),
          pl.BlockSpec((None, N_MOD, D_MODEL), mod_idx),
          pl.BlockSpec((3, D_MODEL), const),
          pl.BlockSpec((tm, GROUP_W), row),
          pl.BlockSpec((tm, GROUP_W), row),
          pl.BlockSpec((tm, GROUP_W), row),
          pl.BlockSpec((tm, GROUP_W), row),
          pl.BlockSpec((tm, GROUP_W), row),
          pl.BlockSpec((tm, GROUP_W), row),
          pl.BlockSpec((tm, G_RANK), gd_idx),
          pl.BlockSpec((G_RANK, GROUP_W), const),
          pl.BlockSpec((D_MODEL, D_MODEL), const),
          pl.BlockSpec((D_MODEL, 2 * D_FF), const),
          pl.BlockSpec((D_FF, D_MODEL), const),
          pl.BlockSpec((1, D_MODEL), const),
      ],
      out_specs=pl.BlockSpec((tm, D_MODEL), out_row),
      out_shape=jax.ShapeDtypeStruct((out_rows, D_MODEL), F32),
      compiler_params=pltpu.CompilerParams(
          dimension_semantics=("arbitrary", "arbitrary"), vmem_limit_bytes=VMEM_LIMIT),
      name="out_ffn_final" if final else "out_ffn",
  )(x, mod, ng, oa, ob, cf, cb, yf, yb, dd, g2, wo, w1, w2, fg)


def _softmax_heads(q, segs, sink_ref, o_ref):
  lo = _lane_lo((1, LANES))
  for s in range(2):
    qs = q[:, s * LANES:(s + 1) * LANES]
    halves = []
    for hf in range(2):
      qm = jnp.where(lo if hf == 0 else jnp.logical_not(lo), qs, jnp.zeros_like(qs))
      scores = []
      for kv, valid in segs:
        sc = _dot_nt(qm, kv[:, 0:LANES])
        if valid is not None:
          sc = jnp.where(valid, sc, NEG_BIG)
        scores.append(sc)
      mx = scores[0].max(axis=-1, keepdims=True)
      for sc in scores[1:]:
        mx = jnp.maximum(mx, sc.max(axis=-1, keepdims=True))
      snk = sink_ref[2 * s + hf:2 * s + hf + 1, 0:1]
      mx = jnp.maximum(mx, snk)
      den = jnp.exp(snk - mx)
      acc = None
      for sc, (kv, _) in zip(scores, segs):
        p = jnp.exp(sc - mx)
        den = den + p.sum(axis=-1, keepdims=True)
        pv = jnp.dot(p.astype(BF16), kv[:, LANES:2 * LANES], preferred_element_type=F32)
        acc = pv if acc is None else acc + pv
      halves.append(acc / den)
    o_ref[:, s * LANES:(s + 1) * LANES] = jnp.where(lo, halves[0], halves[1]).astype(BF16)


def _win_attn_kernel(q_ref, kp_ref, kc_ref, kn_ref, kx_ref, sink_ref, o_ref, *, bpb, cb, nb):
  t = pl.program_id(0) % bpb
  n = t - cb
  is_lat = t >= cb
  big = 4 * BLOCK
  thr_prev = jnp.where(jnp.logical_and(is_lat, n > 0), 0, big)
  thr_cur = jnp.where(is_lat, -big, big)
  thr_next = jnp.where(jnp.logical_and(is_lat, n < nb - 1), 0, big)
  r = lax.broadcasted_iota(jnp.int32, (BLOCK, BLOCK), 0)
  c = lax.broadcasted_iota(jnp.int32, (BLOCK, BLOCK), 1)
  segs = [
      (kp_ref[...], (c - r) >= thr_prev),
      (kc_ref[...], (c - r) >= thr_cur),
      (kn_ref[...], (r - c) >= thr_next),
      (kx_ref[...], None),
  ]
  _softmax_heads(q_ref[...], segs, sink_ref, o_ref)


def _win_attn(qa, kva, sink_tab, dims):
  b, n, ct = dims
  rows = qa.shape[0]
  nb = n // BLOCK
  cb = ct // BLOCK
  bpb = nb + cb

  def prev(i):
    return (jnp.where(i % bpb > cb, i - 1, i), 0)

  def nxt(i):
    t = i % bpb
    return (jnp.where(jnp.logical_and(t >= cb, t < bpb - 1), i + 1, i), 0)

  return pl.pallas_call(
      functools.partial(_win_attn_kernel, bpb=bpb, cb=cb, nb=nb),
      grid=(rows // BLOCK,),
      in_specs=[
          pl.BlockSpec((BLOCK, GROUP_W), lambda i: (i, 0)),
          pl.BlockSpec((BLOCK, GROUP_W), prev),
          pl.BlockSpec((BLOCK, GROUP_W), lambda i: (i, 0)),
          pl.BlockSpec((BLOCK, GROUP_W), nxt),
          pl.BlockSpec((ct, GROUP_W), lambda i: ((i // bpb) * ((n + ct) // ct), 0)),
          pl.BlockSpec((4, LANES), lambda i: (0, 0)),
      ],
      out_specs=pl.BlockSpec((BLOCK, GROUP_W), lambda i: (i, 0)),
      out_shape=jax.ShapeDtypeStruct((rows, GROUP_W), BF16),
      compiler_params=pltpu.CompilerParams(
          dimension_semantics=("arbitrary",), vmem_limit_bytes=VMEM_LIMIT),
      name="window_attn",
  )(qa, kva, kva, kva, kva, sink_tab)


def _glob_attn_kernel(q_ref, kv_ref, bnd_ref, o_ref, m_sc, l_sc, acc_sc, *, ct, ctq, n_chunks):
  lo = _lane_lo((1, LANES))
  qms = []
  for s in range(2):
    qs = q_ref[:, s * LANES:(s + 1) * LANES]
    for hf in range(2):
      qms.append(jnp.where(lo if hf == 0 else jnp.logical_not(lo), qs, jnp.zeros_like(qs)))
  bound = bnd_ref[0:1, 0:1]
  is_ctx = pl.program_id(1) < ctq
  lat_chunks = [(ct + j * KEY_CHUNK, KEY_CHUNK) for j in range(n_chunks)] + [(0, ct)]

  def scores(start, nk):
    k = kv_ref[start:start + nk, 0:LANES]
    return [_dot_nt(k, qm) for qm in qms]

  def v_t(start, nk):
    return kv_ref[pl.ds(start, nk), LANES:2 * LANES].astype(F32).T.astype(BF16)

  def consume(scs, start, nk):
    vt = v_t(start, nk)
    ones = jnp.ones((MOD_ROWS, nk), BF16)
    for h, sc in enumerate(scs):
      hf = h % 2
      p = jnp.exp2(sc - bound).astype(BF16)
      psum = jnp.dot(ones, p, preferred_element_type=F32)
      pv = jnp.dot(vt[hf * HEAD_DIM:(hf + 1) * HEAD_DIM, :], p, preferred_element_type=F32)
      l_sc[h] = l_sc[h] + psum[0:1]
      acc_sc[h] = acc_sc[h] + pv

  def run(chunks):
    scs = scores(*chunks[0])
    for i, ch in enumerate(chunks):
      nxt = scores(*chunks[i + 1]) if i + 1 < len(chunks) else None
      consume(scs, *ch)
      scs = nxt

  l_sc[...] = jnp.zeros(l_sc.shape, F32)
  acc_sc[...] = jnp.zeros(acc_sc.shape, F32)

  @pl.when(is_ctx)
  def _():
    run([(0, ct)])

  @pl.when(jnp.logical_not(is_ctx))
  def _():
    run(lat_chunks)

  @pl.when(jnp.min(l_sc[...]) < UNDERFLOW_GUARD)
  def _():
    m_sc[...] = jnp.full(m_sc.shape, NEG_BIG, F32)
    l_sc[...] = jnp.zeros(l_sc.shape, F32)
    acc_sc[...] = jnp.zeros(acc_sc.shape, F32)

    def online(start, nk):
      k = kv_ref[pl.ds(start, nk), 0:LANES]
      vt = v_t(start, nk)
      for h, qm in enumerate(qms):
        hf = h % 2
        sc = _dot_nt(k, qm)
        m_prev = m_sc[h]
        m_new = jnp.maximum(m_prev, sc.max(axis=0, keepdims=True))
        alpha = jnp.exp2(m_prev - m_new)
        p = jnp.exp2(sc - m_new)
        l_sc[h] = alpha * l_sc[h] + p.sum(axis=0, keepdims=True)
        pv = jnp.dot(vt[hf * HEAD_DIM:(hf + 1) * HEAD_DIM, :], p.astype(BF16),
                     preferred_element_type=F32)
        acc_sc[h] = alpha * acc_sc[h] + pv
        m_sc[h] = m_new

    def body(j, carry):
      online(pl.multiple_of(ct + j * KEY_CHUNK, ct), KEY_CHUNK)
      return carry

    lax.fori_loop(0, jnp.where(is_ctx, 0, n_chunks), body, 0)
    online(0, ct)

  for s in range(2):
    ot = jnp.concatenate([acc_sc[2 * s] / l_sc[2 * s], acc_sc[2 * s + 1] / l_sc[2 * s + 1]], axis=0)
    o_ref[:, s * LANES:(s + 1) * LANES] = ot.T.astype(BF16)


def _score_bound(qk_g):
  gq = jnp.max(jnp.abs(qk_g[0]))
  gk = jnp.max(jnp.abs(qk_g[1]))
  bound = BOUND_MARGIN * HEAD_DIM * gq * gk * (HEAD_DIM ** -0.5 * LOG2E)
  return jnp.broadcast_to(bound, (MOD_ROWS, LANES)).astype(F32)


def _glob_attn(qb, kvb, bound, dims):
  b, n, ct = dims
  tq = Q_TILE
  sa = n + ct
  return pl.pallas_call(
      functools.partial(_glob_attn_kernel, ct=ct, ctq=ct // tq, n_chunks=n // KEY_CHUNK),
      grid=(b, sa // tq),
      in_specs=[
          pl.BlockSpec((tq, GROUP_W), lambda bi, j: (bi * (sa // tq) + j, 0)),
          pl.BlockSpec((sa, GROUP_W), lambda bi, j: (bi, 0)),
          pl.BlockSpec((MOD_ROWS, LANES), lambda bi, j: (0, 0)),
      ],
      out_specs=pl.BlockSpec((tq, GROUP_W), lambda bi, j: (bi * (sa // tq) + j, 0)),
      out_shape=jax.ShapeDtypeStruct((b * sa, GROUP_W), BF16),
      scratch_shapes=[pltpu.VMEM((4, 1, tq), F32), pltpu.VMEM((4, 1, tq), F32),
                      pltpu.VMEM((4, HEAD_DIM, tq), F32)],
      compiler_params=pltpu.CompilerParams(
          dimension_semantics=("arbitrary", "arbitrary"), vmem_limit_bytes=VMEM_LIMIT),
      name="global_attn",
  )(qb, kvb, bound)


def _ret_kernel(rf_ref, rb_ref, gf_ref, gb_ref, dec_ref, lw_ref, ng_ref, of_ref, ob_ref,
                sf_sc, sb_sc):
  j = pl.program_id(1)

  @pl.when(j == 0)
  def _():
    sf_sc[...] = jnp.zeros(sf_sc.shape, F32)
    sb_sc[...] = jnp.zeros(sb_sc.shape, F32)

  lo = _lane_lo((1, LANES))
  rr = lax.broadcasted_iota(jnp.int32, (LANES, LANES), 0) // HEAD_DIM
  cc = lax.broadcasted_iota(jnp.int32, (LANES, LANES), 1) // HEAD_DIM
  blockdiag = rr == cc
  ng = ng_ref[...]

  for d, (r_ref, g_ref, o_ref, s_sc) in enumerate(
      ((rf_ref, gf_ref, of_ref, sf_sc), (rb_ref, gb_ref, ob_ref, sb_sc))):
    for s in range(2):
      sl = slice(s * LANES, (s + 1) * LANES)
      q = r_ref[:, s * LANES:(s + 1) * LANES]
      k = r_ref[:, 256 + s * LANES:256 + (s + 1) * LANES]
      v = r_ref[:, 512 + s * LANES:512 + (s + 1) * LANES]
      qw = lw_ref[d, 0][:, sl]
      kw = lw_ref[d, 1][:, sl]
      cd = lw_ref[d, 2][0:1, sl]
      st = s_sc[s]
      o = _dot(q.astype(F32) * qw, st)
      halves = []
      for hf in range(2):
        qm = jnp.where(lo if hf == 0 else jnp.logical_not(lo), q, jnp.zeros_like(q))
        att = _dot_nt(qm, k) * dec_ref[d, 2 * s + hf]
        halves.append(_dot(att, v))
      o = o + jnp.where(lo, halves[0], halves[1])
      u = _dot_tn(k.astype(F32) * kw, v)
      s_sc[s] = st * cd + jnp.where(blockdiag, u, 0.0)
      gate = g_ref[:, d * GROUP_W + s * LANES:d * GROUP_W + (s + 1) * LANES]
      o_ref[:, sl] = _head_norm_slab(o, lo) * ng[0:1, sl] * _silu(gate)


def _scan_maps(bpb, ncx):
  def fwd(j):
    return j

  def bwd(j):
    return jnp.where(j < ncx, ncx - 1 - j, bpb - 1 - (j - ncx))

  return fwd, bwd


def _retention(rc, gc, dec, lw, ng, dims):
  b, n, ct = dims
  c = RET_CHUNK
  rows = rc.shape[0]
  bpb = (n + ct) // c
  f, r = _scan_maps(bpb, ct // c)
  fwd = lambda bi, j: (bi * bpb + f(j), 0)
  bwd = lambda bi, j: (bi * bpb + r(j), 0)

  return pl.pallas_call(
      _ret_kernel,
      grid=(b, bpb),
      in_specs=[
          pl.BlockSpec((c, 768), fwd),
          pl.BlockSpec((c, 768), bwd),
          pl.BlockSpec((c, 512), fwd),
          pl.BlockSpec((c, 512), bwd),
          pl.BlockSpec((2, 4, c, c), lambda bi, j: (0, 0, 0, 0)),
          pl.BlockSpec((2, 3, c, GROUP_W), lambda bi, j: (0, 0, 0, 0)),
          pl.BlockSpec((1, GROUP_W), lambda bi, j: (0, 0)),
      ],
      out_specs=[pl.BlockSpec((c, GROUP_W), fwd), pl.BlockSpec((c, GROUP_W), bwd)],
      out_shape=[jax.ShapeDtypeStruct((rows, GROUP_W), F32)] * 2,
      scratch_shapes=[pltpu.VMEM((2, LANES, LANES), F32)] * 2,
      compiler_params=pltpu.CompilerParams(
          dimension_semantics=("arbitrary", "arbitrary"), vmem_limit_bytes=VMEM_LIMIT),
      name="retention",
  )(rc, rc, gc, gc, dec, lw, ng)


def _rwkv_kernel(df_ref, db_ref, mu_ref, w0_ref, w2_ref, a0_ref, a2_ref, rho_ref, kk_ref, ka_ref,
                 lng_ref, lnb_ref, yf_ref, yb_ref, s_sc, z_sc, *, ncx, nbatch):
  j = pl.program_id(0)
  c = RWKV_CHUNK
  sr = N_HEADS * c
  at_start = jnp.logical_or(j == 0, j == ncx)

  @pl.when(j == 0)
  def _():
    s_sc[...] = jnp.zeros(s_sc.shape, F32)

  @pl.when(at_start)
  def _():
    z_sc[:, :, 0:8, :] = jnp.zeros((2, nbatch, 8, z_sc.shape[3]), F32)
    z_sc[:, :, c + 8:c + 16, :] = jnp.zeros((2, nbatch, 8, z_sc.shape[3]), F32)

  lane_head = lax.broadcasted_iota(jnp.int32, (1, GROUP_W), 1) // HEAD_DIM
  head_masks = [lane_head == h for h in range(N_HEADS)]
  rr = lax.broadcasted_iota(jnp.int32, (GROUP_W, GROUP_W), 0) // HEAD_DIM
  cc = lax.broadcasted_iota(jnp.int32, (GROUP_W, GROUP_W), 1) // HEAD_DIM
  blockdiag = rr == cc
  ti = lax.broadcasted_iota(jnp.int32, (c, c), 0)
  si = lax.broadcasted_iota(jnp.int32, (c, c), 1)
  ts = lax.broadcasted_iota(jnp.int32, (sr, sr), 0)
  ss = lax.broadcasted_iota(jnp.int32, (sr, sr), 1)
  eye = jnp.where(ts == ss, 1.0, 0.0)
  ts, ss = ts % c, ss % c
  kkp = kk_ref[...]
  kap = ka_ref[...]
  lng = lng_ref[...]
  lnb = lnb_ref[...]
  lo = _lane_lo((1, LANES))

  def stack(x):
    return jnp.concatenate([jnp.where(m, x, 0.0) for m in head_masks], axis=0)

  def unstack(x):
    return x[0:c] + x[c:2 * c] + x[2 * c:3 * c] + x[3 * c:4 * c]

  chains = [(d, bi) for d in range(2) for bi in range(nbatch)]
  st = {}

  for ch in chains:
    d, bi = ch
    d_ref = df_ref if d == 0 else db_ref
    upto_c = (si <= ti) if d == 0 else (si >= ti)
    z = jnp.concatenate([d_ref[bi, :, 0:768], d_ref[bi, :, 896 + d * LANES:1024 + d * LANES]], axis=-1)
    z_sc[d, bi, 8:c + 8, :] = z
    zs = z_sc[d, bi, 7:c + 7, :] if d == 0 else z_sc[d, bi, 9:c + 9, :]
    zm = z + mu_ref[d:d + 1, :] * (zs - z)
    if d == 0:
      z_sc[d, bi, 7:8, :] = z[c - 1:c, :]
    else:
      z_sc[d, bi, c + 8:c + 9, :] = z[0:1, :]
    r = zm[:, 0:256]
    k = zm[:, 256:512]
    v = zm[:, 512:768]
    wa = zm[:, 768:896]
    logw = -DECAY_SCALE * jax.nn.sigmoid(w0_ref[d:d + 1, :] + _dot(jnp.tanh(wa), w2_ref[d]))
    ag = jax.nn.sigmoid(a0_ref[d:d + 1, :] + _dot(wa, a2_ref[d]))
    kkr = k * kkp
    kkn = kkr / jnp.maximum(jnp.sqrt(_head_sum(kkr * kkr)), 1e-12)
    kt = k * (1.0 + (ag - 1.0) * kap)
    bonus = _head_sum(r * kt * rho_ref[d:d + 1, :]) * v
    cum = jnp.dot(jnp.where(upto_c, 1.0, 0.0), logw, precision=HIGHEST, preferred_element_type=F32)
    tot = cum[c - 1:c, :] if d == 0 else cum[0:1, :]
    e_neg = jnp.exp(-cum)
    e_rest = jnp.exp(tot - cum)
    bvec = kkn * ag
    at = -kkn * jnp.exp(cum - logw)
    rt = r * jnp.exp(cum)
    st[ch] = dict(
        v=v, bonus=bonus, g_all=jnp.exp(tot), at=at, rt=rt,
        x=jnp.concatenate([stack(at), stack(rt)], axis=0),
        y=jnp.concatenate([stack(bvec * e_neg), stack(kt * e_neg)], axis=0),
        bk=jnp.concatenate([bvec * e_rest, kt * e_rest], axis=0),
        vm=stack(v))

  for ch in chains:
    d, bi = ch
    e = st[ch]
    before = (ss < ts) if d == 0 else (ss > ts)
    upto = (ss <= ts) if d == 0 else (ss >= ts)
    a_all = _dot_nt(e["x"], e["y"])
    e["lab"] = jnp.where(before, a_all[0:sr, 0:sr], 0.0)
    lak = jnp.where(before, a_all[0:sr, sr:2 * sr], 0.0)
    e["p"] = jnp.concatenate([jnp.where(upto, a_all[sr:2 * sr, 0:sr], 0.0),
                              jnp.where(upto, a_all[sr:2 * sr, sr:2 * sr], 0.0)], axis=-1)
    s0 = s_sc[d, bi]
    e["s0"] = s0
    ar = _dot_nt(jnp.concatenate([e["at"], e["rt"]], axis=0), s0)
    e["rh"] = ar[c:2 * c]
    e["xs"] = stack(ar[0:c]) + _dot(lak, e["vm"])

  for ch in chains:
    e = st[ch]
    e["lp"] = _dot(e["lab"], e["lab"])
    e["t"] = eye + e["lab"]
  for it in range(4):
    for ch in chains:
      e = st[ch]
      lp = e["lp"]
      e["t"] = e["t"] + _dot(e["t"], lp)
      if it < 3:
        e["lp"] = _dot(lp, lp)

  for ch in chains:
    e = st[ch]
    e["us"] = _dot(e["t"], e["xs"])
  for ch in chains:
    d, bi = ch
    e = st[ch]
    ys = _dot(e["p"], jnp.concatenate([e["us"], e["vm"]], axis=0))
    y = e["rh"] + unstack(ys)
    u = unstack(e["us"])
    upd = _dot_tn(jnp.concatenate([u, e["v"]], axis=0), e["bk"])
    s_sc[d, bi] = e["s0"] * e["g_all"] + jnp.where(blockdiag, upd, 0.0)
    y_ref = yf_ref if d == 0 else yb_ref
    for s in range(2):
      sl = slice(s * LANES, (s + 1) * LANES)
      y_ref[bi, :, sl] = (_head_norm_slab(y[:, sl], lo) * lng[0:1, sl] + lnb[0:1, sl]
                          + e["bonus"][:, sl])


def _rwkv(dd, mu, w0, w2p, a0, a2p, rho, kk, ka, lng, lnb, dims):
  b, n, ct = dims
  c = RWKV_CHUNK
  sa = n + ct
  bpb = sa // c
  ncx = ct // c
  zw = 768 + LANES
  f, r = _scan_maps(bpb, ncx)
  fwd = lambda j: (0, f(j), 0)
  bwd = lambda j: (0, r(j), 0)
  const2 = lambda j: (0, 0)
  const3 = lambda j: (0, 0, 0)
  d3 = dd.reshape(b, sa, D_COLS)

  yf, yb = pl.pallas_call(
      functools.partial(_rwkv_kernel, ncx=ncx, nbatch=b),
      grid=(bpb,),
      in_specs=[
          pl.BlockSpec((b, c, D_COLS), fwd),
          pl.BlockSpec((b, c, D_COLS), bwd),
          pl.BlockSpec((2, zw), const2),
          pl.BlockSpec((2, GROUP_W), const2),
          pl.BlockSpec((2, LANES, GROUP_W), const3),
          pl.BlockSpec((2, GROUP_W), const2),
          pl.BlockSpec((2, LANES, GROUP_W), const3),
          pl.BlockSpec((2, GROUP_W), const2),
          pl.BlockSpec((1, GROUP_W), const2),
          pl.BlockSpec((1, GROUP_W), const2),
          pl.BlockSpec((1, GROUP_W), const2),
          pl.BlockSpec((1, GROUP_W), const2),
      ],
      out_specs=[pl.BlockSpec((b, c, GROUP_W), fwd), pl.BlockSpec((b, c, GROUP_W), bwd)],
      out_shape=[jax.ShapeDtypeStruct((b, sa, GROUP_W), F32)] * 2,
      scratch_shapes=[pltpu.VMEM((2, b, GROUP_W, GROUP_W), F32), pltpu.VMEM((2, b, c + 16, zw), F32)],
      compiler_params=pltpu.CompilerParams(
          dimension_semantics=("arbitrary",), vmem_limit_bytes=VMEM_LIMIT),
      name="rwkv7",
  )(d3, d3, mu, w0, w2p, a0, a2p, rho, kk, ka, lng, lnb)
  return yf.reshape(b * sa, GROUP_W), yb.reshape(b * sa, GROUP_W)


_Q_HEAD_ORDER = (0, 2, 1, 3)


def _q_perm():
  return np.concatenate([np.arange(h * HEAD_DIM, (h + 1) * HEAD_DIM) for h in _Q_HEAD_ORDER])


def _proj_col_perm():
  qp = _q_perm()
  cols = [qp, np.arange(256, 512), 512 + qp, np.arange(768, 1024), np.arange(1024, 2304)]
  d0 = 2304
  cols.append(np.arange(d0, d0 + 768 + 128))
  wdf, wdb, adf, adb = (np.arange(d0 + 896 + 64 * t, d0 + 896 + 64 * (t + 1)) for t in range(4))
  cols += [wdf, adf, wdb, adb]
  return np.concatenate(cols)


def _rope_tables(n, tail):
  def cos_sin(pos, dim):
    inv = 1.0 / (ROPE_BASE ** (jnp.arange(0, dim, 2, dtype=F32) / dim))
    ang = pos.astype(F32)[:, None] * inv[None, :]
    return jnp.cos(ang), jnp.sin(ang)

  rows = n // GRID_W
  row = jnp.repeat(jnp.arange(rows), GRID_W)
  col = jnp.arange(rows * GRID_W) % GRID_W
  cr, sr = cos_sin(row, HEAD_DIM // 2)
  cc, sc = cos_sin(col, HEAD_DIM // 2)
  cq, sq = cos_sin(jnp.arange(n), HEAD_DIM)
  cax = jnp.concatenate([cr, cr, cc, cc] * 2, axis=-1)
  sax = jnp.concatenate([-sr, sr, -sc, sc] * 2, axis=-1)
  csq = jnp.concatenate([cq, cq] * 2, axis=-1)
  ssq = jnp.concatenate([-sq, sq] * 2, axis=-1)
  ones = jnp.ones((tail, LANES), F32)
  zeros = jnp.zeros((tail, LANES), F32)
  return (jnp.concatenate([cax, ones]), jnp.concatenate([sax, zeros]),
          jnp.concatenate([csq, ones]), jnp.concatenate([ssq, zeros]))


def _retention_tables():
  c = RET_CHUNK
  lg = jnp.log1p(-jnp.exp2(-5.0 - jnp.arange(4, dtype=F32)))
  idx = jnp.arange(c, dtype=F32)
  rel = idx[:, None] - idx[None, :]
  dfw = jnp.where(rel[None] >= 0, jnp.exp(jnp.maximum(rel, 0.0)[None] * lg[:, None, None]), 0.0)
  dec = jnp.stack([dfw, jnp.swapaxes(dfw, 1, 2)])
  lane_lg = jnp.repeat(lg, HEAD_DIM)[None, :]
  qw_f = jnp.exp((idx + 1.0)[:, None] * lane_lg)
  kw_f = jnp.exp((c - 1.0 - idx)[:, None] * lane_lg)
  qw_b = jnp.exp((c - idx)[:, None] * lane_lg)
  kw_b = jnp.exp(idx[:, None] * lane_lg)
  cd = jnp.broadcast_to(jnp.exp(c * lane_lg), (c, GROUP_W))
  lw = jnp.stack([jnp.stack([qw_f, kw_f, cd]), jnp.stack([qw_b, kw_b, cd])])
  return dec, lw


def kernel(x, c, ctx, c_ctx, w_mod, b_mod, norm_g, ffn_w_in, ffn_w_out, w_in, w_out, attn_sink,
           qk_norm_g, ret_norm_g, rwkv_mu, rwkv_w0, rwkv_w2, rwkv_a0, rwkv_a2, rwkv_rho, rwkv_k_k,
           rwkv_k_a, rwkv_g2, rwkv_ln_g, rwkv_ln_b, final_norm_g):
  b, n, d = x.shape
  ct = ctx.shape[1]
  depth = w_mod.shape[0]
  assert d == D_MODEL and b < MOD_ROWS
  assert n % ROW_TILE == 0 and ct % ROW_TILE == 0 and n % KEY_CHUNK == 0 and ct % Q_TILE == 0
  assert n % RET_CHUNK == 0 and ct % RET_CHUNK == 0 and n % GRID_W == 0 and KEY_CHUNK % ct == 0
  dims = (b, n, ct)

  cs = jnp.zeros((MOD_ROWS, d), F32).at[:b].set(c).at[b].set(c_ctx)
  mod = _modulation(cs, w_mod, b_mod).reshape(depth, MOD_ROWS, N_MOD, d)

  tabs = _rope_tables(n, ROW_TILE)
  dec, lw = _retention_tables()
  col_perm = _proj_col_perm()
  q_perm = _q_perm()
  out_perm = np.concatenate([q_perm, 256 + q_perm, np.arange(512, 1024)])
  zpad = jnp.zeros((2, A_RANK, GROUP_W), F32)

  xs = jnp.concatenate([ctx, x], axis=1).reshape(b * (ct + n), d)
  for l in range(depth):
    last = l == depth - 1
    w1a, w2a = ffn_w_in[l, 0].astype(BF16), ffn_w_out[l, 0].astype(BF16)
    w1b, w2b = ffn_w_in[l, 1].astype(BF16), ffn_w_out[l, 1].astype(BF16)
    win = w_in[l][:, col_perm].astype(BF16)
    wo = w_out[l][out_perm, :].astype(BF16)
    qkg = jnp.tile(qk_norm_g[l], (1, GROUP_W // HEAD_DIM))
    sink_tab = jnp.broadcast_to(attn_sink[l][jnp.asarray(_Q_HEAD_ORDER)][:, None], (4, LANES))
    w2p = jnp.concatenate([rwkv_w2[l], zpad], axis=1)
    a2p = jnp.concatenate([zpad, rwkv_a2[l]], axis=1)

    xs, qa, kva, qb, kvb, rc, gc, dd = _ffn_proj(
        xs, mod[l], norm_g[l], w1a, w2a, win, qkg, tabs, dims)
    oa = _win_attn(qa, kva, sink_tab, dims)
    ob = _glob_attn(qb, kvb, _score_bound(qk_norm_g[l]), dims)
    cf, cb = _retention(rc, gc, dec, lw, ret_norm_g[l][None, :], dims)
    yf, yb = _rwkv(dd, rwkv_mu[l], rwkv_w0[l], w2p, rwkv_a0[l], a2p,
                   rwkv_rho[l].reshape(2, GROUP_W), rwkv_k_k[l][None, :], rwkv_k_a[l][None, :],
                   rwkv_ln_g[l][None, :], rwkv_ln_b[l][None, :], dims)
    xs = _out_ffn(xs, mod[l], norm_g[l], oa, ob, cf, cb, yf, yb, dd, rwkv_g2[l].astype(BF16), wo,
                  w1b, w2b, final_norm_g[None, :], dims, final=last)
  return xs.reshape(b, n, d)
```

```python
import functools

import numpy as np
import jax
import jax.numpy as jnp
from jax import lax
from jax.experimental import pallas as pl
from jax.experimental.pallas import tpu as pltpu

F32 = jnp.float32
BF16 = jnp.bfloat16

D_MODEL = 1024
HEAD_DIM = 64
N_HEADS = 4
GROUP_W = 256
KV_W = 128
GRID_W = 64
BLOCK = 128
D_FF = 2816
A_RANK = 64
G_RANK = 128
N_MOD = 9
ROPE_BASE = 10000.0
RMS_EPS = 1e-6
GN_EPS = 64e-5
DECAY_SCALE = 0.6065306597126334
PROJ_COLS = 3456
D_COLS = 1152
LANES = 128
MOD_ROWS = 8
BF16_ROWS = 16

ROW_TILE = 256
FF_CHUNK = 1408
Q_TILE = 256
WIN_Q = 2 * BLOCK
KEY_CHUNK = 1024
RET_CHUNK = 256
RWKV_CHUNK = 32
VMEM_LIMIT = 56 * 1024 * 1024

NEG_BIG = -1e30
LOG2E = 1.4426950408889634
BOUND_MARGIN = 1.02
UNDERFLOW_GUARD = 2.0 ** -100
HIGHEST = lax.Precision.HIGHEST


def _dot(a, b):
  return jnp.dot(a.astype(BF16), b.astype(BF16), preferred_element_type=F32)


def _dot_nt(a, b):
  return lax.dot_general(a.astype(BF16), b.astype(BF16), (((1,), (1,)), ((), ())),
                         preferred_element_type=F32)


def _dot_tn(a, b):
  return jnp.dot(a.astype(F32).T.astype(BF16), b.astype(BF16), preferred_element_type=F32)


def _silu(x):
  return x * jax.nn.sigmoid(x)


def _lane_lo(shape):
  return (lax.broadcasted_iota(jnp.int32, shape, len(shape) - 1) % LANES) < HEAD_DIM


def _half_sum(x, lo):
  s_lo = jnp.sum(jnp.where(lo, x, 0.0), axis=-1, keepdims=True)
  s_all = jnp.sum(x, axis=-1, keepdims=True)
  return jnp.where(lo, s_lo, s_all - s_lo)


def _head_sum(x):
  lo = _lane_lo((1, LANES))
  return jnp.concatenate([_half_sum(x[:, o:o + LANES], lo) for o in range(0, GROUP_W, LANES)], axis=-1)


def _head_norm_slab(y, lo):
  mu = _half_sum(y, lo) * (1.0 / HEAD_DIM)
  d = y - mu
  var = _half_sum(d * d, lo) * (1.0 / HEAD_DIM)
  return d * lax.rsqrt(var + GN_EPS)


def _rope_slab(x, c, s, half):
  first = (lax.broadcasted_iota(jnp.int32, x.shape, 1) % (2 * half)) < half
  partner = jnp.where(first, pltpu.roll(x, LANES - half, axis=1), pltpu.roll(x, half, axis=1))
  return x * c + partner * s


def _mod_kernel(c_ref, w_ref, b_ref, o_ref):
  s = _silu(c_ref[...])
  o_ref[...] = jnp.dot(s, w_ref[...], precision=HIGHEST, preferred_element_type=F32) + b_ref[...]


def _modulation(cs, w_mod, b_mod):
  depth = w_mod.shape[0]
  return pl.pallas_call(
      _mod_kernel,
      grid=(depth, N_MOD),
      in_specs=[
          pl.BlockSpec((MOD_ROWS, D_MODEL), lambda l, j: (0, 0)),
          pl.BlockSpec((None, D_MODEL, D_MODEL), lambda l, j: (l, 0, j)),
          pl.BlockSpec((None, 1, D_MODEL), lambda l, j: (l, 0, j)),
      ],
      out_specs=pl.BlockSpec((None, MOD_ROWS, D_MODEL), lambda l, j: (l, 0, j)),
      out_shape=jax.ShapeDtypeStruct((depth, MOD_ROWS, N_MOD * D_MODEL), F32),
      compiler_params=pltpu.CompilerParams(
          dimension_semantics=("arbitrary", "arbitrary"), vmem_limit_bytes=VMEM_LIMIT),
      name="modulation",
  )(cs, w_mod, b_mod.reshape(depth, 1, N_MOD * D_MODEL))


def _rms_mod(x, g, shift, scale):
  h = x * lax.rsqrt(jnp.mean(x * x, axis=-1, keepdims=True) + RMS_EPS) * g
  return h * (1.0 + scale) + shift


def _swiglu(h, w1_ref, w2_ref):
  hb = h.astype(BF16)
  acc = None
  for c0 in range(0, D_FF, FF_CHUNK):
    u1 = jnp.dot(hb, w1_ref[:, c0:c0 + FF_CHUNK], preferred_element_type=F32)
    u2 = jnp.dot(hb, w1_ref[:, D_FF + c0:D_FF + c0 + FF_CHUNK], preferred_element_type=F32)
    a = (_silu(u1) * u2).astype(BF16)
    part = jnp.dot(a, w2_ref[c0:c0 + FF_CHUNK, :], preferred_element_type=F32)
    acc = part if acc is None else acc + part
  return acc


def _head_rms(x, g):
  lo = _lane_lo((1, LANES))
  sq = x * x
  ms = jnp.concatenate([_half_sum(sq[:, o:o + LANES], lo) for o in range(0, x.shape[-1], LANES)],
                       axis=-1) * (1.0 / HEAD_DIM)
  return x * lax.rsqrt(ms + RMS_EPS) * g


def _split3(x):
  hi = x.astype(BF16)
  r1 = x - hi.astype(F32)
  mid = r1.astype(BF16)
  lo = (r1 - mid.astype(F32)).astype(BF16)
  return hi, mid, lo


def _rope_wide(x, c, s, half):
  parts = [_rope_slab(x[:, o:o + LANES], c, s, half) for o in range(0, x.shape[-1], LANES)]
  return parts[0] if len(parts) == 1 else jnp.concatenate(parts, axis=-1)


def _ffn_proj_kernel(x_ref, mod_ref, ng_ref, w1_ref, w2_ref, win_ref, qkg_ref,
                     cax_ref, sax_ref, csq_ref, ssq_ref,
                     xo_ref, qa_ref, kva_ref, qb_ref, kvb_ref, rc_ref, gc_ref, dd_ref):
  x = x_ref[...]
  m = mod_ref[...]
  ng = ng_ref[...]
  h = _rms_mod(x, ng[0:1], m[0:1], m[1:2])
  x1 = x + 0.5 * m[2:3] * _swiglu(h, w1_ref, w2_ref)
  xo_ref[...] = x1
  hb = _rms_mod(x1, ng[1:2], m[3:4], m[4:5]).astype(BF16)

  cax, sax = cax_ref[...], sax_ref[...]
  csq, ssq = csq_ref[...], ssq_ref[...]
  qkg = qkg_ref[...]
  scale = HEAD_DIM ** -0.5

  p = jnp.dot(hb, win_ref[:, 0:512], preferred_element_type=F32)
  qa_ref[...] = (_rope_wide(p[:, 0:256], cax, sax, 16) * (scale * LOG2E)).astype(BF16)
  kva_ref[:, 0:128] = _rope_wide(p[:, 256:384], cax, sax, 16).astype(BF16)
  kva_ref[:, 128:256] = p[:, 384:512].astype(BF16)
  p = jnp.dot(hb, win_ref[:, 512:1024], preferred_element_type=F32)
  qn = _head_rms(p[:, 0:256], qkg[0:1, :])
  kn = _head_rms(p[:, 256:384], qkg[1:2, 0:128])
  qb_ref[...] = (_rope_wide(qn, cax, sax, 16) * (scale * LOG2E)).astype(BF16)
  kvb_ref[:, 0:128] = _rope_wide(kn, cax, sax, 16).astype(BF16)
  kvb_ref[:, 128:256] = p[:, 384:512].astype(BF16)
  p = jnp.dot(hb, win_ref[:, 1024:2304], preferred_element_type=F32)
  rc_ref[:, 0:256] = _rope_wide(p[:, 0:256], csq, ssq, 32).astype(BF16)
  rc_ref[:, 256:512] = (_rope_wide(p[:, 256:512], csq, ssq, 32) * scale).astype(BF16)
  rc_ref[:, 512:768] = p[:, 512:768].astype(BF16)
  gc_ref[...] = p[:, 768:1280]
  dd_ref[...] = jnp.dot(hb, win_ref[:, 2304:3456], preferred_element_type=F32)


def _ffn_proj(x, mod, ng, w1, w2, win, qkg, tabs, dims):
  b, n, ct = dims
  rows = x.shape[0]
  tm = ROW_TILE
  tpb = (n + ct) // tm
  ctt = ct // tm
  ident = n // tm

  def row(i):
    return (i, 0)

  def const(i):
    return (0, 0)

  def mod_idx(i):
    return (jnp.where(i % tpb < ctt, b, i // tpb), 0, 0)

  def tab_idx(i):
    t = i % tpb
    return (jnp.where(t < ctt, ident, t - ctt), 0)

  widths = (D_MODEL, 256, 256, 256, 256, 768, 512, D_COLS)
  dtypes = (F32, BF16, BF16, BF16, BF16, BF16, F32, F32)
  return pl.pallas_call(
      _ffn_proj_kernel,
      grid=(rows // tm,),
      in_specs=[
          pl.BlockSpec((tm, D_MODEL), row),
          pl.BlockSpec((None, N_MOD, D_MODEL), mod_idx),
          pl.BlockSpec((3, D_MODEL), const),
          pl.BlockSpec((D_MODEL, 2 * D_FF), const),
          pl.BlockSpec((D_FF, D_MODEL), const),
          pl.BlockSpec((D_MODEL, PROJ_COLS), const),
          pl.BlockSpec((2, GROUP_W), const),
          pl.BlockSpec((tm, LANES), tab_idx),
          pl.BlockSpec((tm, LANES), tab_idx),
          pl.BlockSpec((tm, LANES), tab_idx),
          pl.BlockSpec((tm, LANES), tab_idx),
      ],
      out_specs=[pl.BlockSpec((tm, w), row) for w in widths],
      out_shape=[jax.ShapeDtypeStruct((rows, w), dt) for w, dt in zip(widths, dtypes)],
      compiler_params=pltpu.CompilerParams(
          dimension_semantics=("arbitrary",), vmem_limit_bytes=VMEM_LIMIT),
      name="ffn_proj",
  )(x, mod, ng, w1, w2, win, qkg, *tabs)


def _out_ffn_kernel(x_ref, mod_ref, ng_ref, oa_ref, ob_ref, cf_ref, cb_ref, yf_ref, yb_ref,
                    gd_ref, g2_ref, wo_ref, w1_ref, w2_ref, fg_ref, xo_ref, *, final):
  x = x_ref[...]
  m = mod_ref[...]
  ng = ng_ref[...]
  gate = _dot(jax.nn.sigmoid(gd_ref[...]), g2_ref[...])
  od = (yf_ref[...] + yb_ref[...]) * gate
  oc = cf_ref[...] + cb_ref[...]
  y = jnp.dot(oa_ref[...], wo_ref[0:256, :], preferred_element_type=F32)
  y = y + jnp.dot(ob_ref[...], wo_ref[256:512, :], preferred_element_type=F32)
  y = y + jnp.dot(oc.astype(BF16), wo_ref[512:768, :], preferred_element_type=F32)
  y = y + jnp.dot(od.astype(BF16), wo_ref[768:1024, :], preferred_element_type=F32)
  x1 = x + m[5:6] * y
  h = _rms_mod(x1, ng[2:3], m[6:7], m[7:8])
  x2 = x1 + 0.5 * m[8:9] * _swiglu(h, w1_ref, w2_ref)
  if final:
    x2 = x2 * lax.rsqrt(jnp.mean(x2 * x2, axis=-1, keepdims=True) + RMS_EPS) * fg_ref[...]
  xo_ref[...] = x2


def _out_ffn(x, mod, ng, oa, ob, cf, cb, yf, yb, dd, g2, wo, w1, w2, fg, dims, final):
  b, n, ct = dims
  tm = ROW_TILE
  tpb = (n + ct) // tm
  ctt = ct // tm
  if final:
    grid = (b, n // tm)
    row = lambda bi, j: (bi * tpb + ctt + j, 0)
    out_row = lambda bi, j: (bi * (n // tm) + j, 0)
    mod_idx = lambda bi, j: (bi, 0, 0)
    gd_idx = lambda bi, j: (bi * tpb + ctt + j, 768 // G_RANK)
    out_rows = b * n
  else:
    grid = (b, tpb)
    row = lambda bi, j: (bi * tpb + j, 0)
    out_row = row
    mod_idx = lambda bi, j: (jnp.where(j < ctt, b, bi), 0, 0)
    gd_idx = lambda bi, j: (bi * tpb + j, 768 // G_RANK)
    out_rows = x.shape[0]
  const = lambda bi, j: (0, 0)

  return pl.pallas_call(
      functools.partial(_out_ffn_kernel, final=final),
      grid=grid,
      in_specs=[
          pl.BlockSpec((tm, D_MODEL), row),
          pl.BlockSpec((None, N_MOD, D_MODEL), mod_idx),
          pl.BlockSpec((3, D_MODEL), const),
          pl.BlockSpec((tm, GROUP_W), row),
          pl.BlockSpec((tm, GROUP_W), row),
          pl.BlockSpec((tm, GROUP_W), row),
          pl.BlockSpec((tm, GROUP_W), row),
          pl.BlockSpec((tm, GROUP_W), row),
          pl.BlockSpec((tm, GROUP_W), row),
          pl.BlockSpec((tm, G_RANK), gd_idx),
          pl.BlockSpec((G_RANK, GROUP_W), const),
          pl.BlockSpec((D_MODEL, D_MODEL), const),
          pl.BlockSpec((D_MODEL, 2 * D_FF), const),
          pl.BlockSpec((D_FF, D_MODEL), const),
          pl.BlockSpec((1, D_MODEL), const),
      ],
      out_specs=pl.BlockSpec((tm, D_MODEL), out_row),
      out_shape=jax.ShapeDtypeStruct((out_rows, D_MODEL), F32),
      compiler_params=pltpu.CompilerParams(
          dimension_semantics=("arbitrary", "arbitrary"), vmem_limit_bytes=VMEM_LIMIT),
      name="out_ffn_final" if final else "out_ffn",
  )(x, mod, ng, oa, ob, cf, cb, yf, yb, dd, g2, wo, w1, w2, fg)


def _win_attn_kernel(q_ref, k0_ref, k1_ref, k2_ref, k3_ref, kx_ref, sink_ref, o_ref, *, spb, cs, nb):
  t = pl.program_id(0) % spb
  is_lat = t >= cs
  n = 2 * (t - cs)
  big = 4 * BLOCK
  off_lat = jnp.where(is_lat, 0, 2 * big)
  off0 = jnp.where(jnp.logical_and(is_lat, n > 0), 0, 2 * big)
  off3 = jnp.where(jnp.logical_and(is_lat, n + 2 < nb), 0, 2 * big)
  r = lax.broadcasted_iota(jnp.int32, (BLOCK, WIN_Q), 0)
  lane = lax.broadcasted_iota(jnp.int32, (BLOCK, WIN_Q), 1)
  d = r - lane % BLOCK
  first = lane < BLOCK
  segs = [
      (k0_ref, d >= jnp.where(first, 0, big) + off0),
      (k1_ref, d >= jnp.where(first, -big, 0) + off_lat),
      (k2_ref, -d >= jnp.where(first, 0, -big) + off_lat),
      (k3_ref, -d >= jnp.where(first, big, 0) + off3),
      (kx_ref, None),
  ]
  lo = _lane_lo((1, LANES))
  qms = []
  for s in range(2):
    qs = q_ref[:, s * LANES:(s + 1) * LANES]
    for hf in range(2):
      qms.append(jnp.where(lo if hf == 0 else jnp.logical_not(lo), qs, jnp.zeros_like(qs)))
  all_scores = []
  for qm in qms:
    scs = []
    for ref, valid in segs:
      sc = _dot_nt(ref[:, 0:LANES], qm)
      scs.append(sc if valid is None else jnp.where(valid, sc, NEG_BIG))
    all_scores.append(scs)
  vts = [ref[:, LANES:2 * LANES].astype(F32).T.astype(BF16) for ref, _ in segs]
  outs = []
  for h, scs in enumerate(all_scores):
    hf = h % 2
    snk = sink_ref[h:h + 1, 0:1]
    mx = jnp.maximum(scs[0].max(axis=0, keepdims=True), snk)
    for sc in scs[1:]:
      mx = jnp.maximum(mx, sc.max(axis=0, keepdims=True))
    den = jnp.exp2(snk - mx)
    acc = None
    for sc, vt in zip(scs, vts):
      p = jnp.exp2(sc - mx)
      den = den + p.sum(axis=0, keepdims=True)
      pv = jnp.dot(vt[hf * HEAD_DIM:(hf + 1) * HEAD_DIM, :], p.astype(BF16),
                   preferred_element_type=F32)
      acc = pv if acc is None else acc + pv
    outs.append(acc / den)
  for s in range(2):
    ot = jnp.concatenate([outs[2 * s], outs[2 * s + 1]], axis=0)
    o_ref[:, s * LANES:(s + 1) * LANES] = ot.T.astype(BF16)


def _win_attn(qa, kva, sink_tab, dims):
  b, n, ct = dims
  rows = qa.shape[0]
  nb = n // BLOCK
  cb = ct // BLOCK
  bpb = nb + cb
  spb = (n + ct) // WIN_Q
  cs = ct // WIN_Q

  def key_block(j):
    def idx(i):
      blk = cb + 2 * (i % spb - cs) + j
      return ((i // spb) * bpb + jnp.clip(blk, cb, bpb - 1), 0)
    return idx

  return pl.pallas_call(
      functools.partial(_win_attn_kernel, spb=spb, cs=cs, nb=nb),
      grid=(rows // WIN_Q,),
      in_specs=[pl.BlockSpec((WIN_Q, GROUP_W), lambda i: (i, 0))]
      + [pl.BlockSpec((BLOCK, GROUP_W), key_block(j)) for j in (-1, 0, 1, 2)]
      + [pl.BlockSpec((ct, GROUP_W), lambda i: ((i // spb) * ((n + ct) // ct), 0)),
         pl.BlockSpec((4, LANES), lambda i: (0, 0))],
      out_specs=pl.BlockSpec((WIN_Q, GROUP_W), lambda i: (i, 0)),
      out_shape=jax.ShapeDtypeStruct((rows, GROUP_W), BF16),
      compiler_params=pltpu.CompilerParams(
          dimension_semantics=("arbitrary",), vmem_limit_bytes=VMEM_LIMIT),
      name="window_attn",
  )(qa, kva, kva, kva, kva, kva, sink_tab)


def _glob_attn_kernel(q_ref, kv_ref, bnd_ref, o_ref, m_sc, l_sc, acc_sc, *, ct, ctq, n_chunks):
  lo = _lane_lo((1, LANES))
  qms = []
  for s in range(2):
    qs = q_ref[:, s * LANES:(s + 1) * LANES]
    for hf in range(2):
      qms.append(jnp.where(lo if hf == 0 else jnp.logical_not(lo), qs, jnp.zeros_like(qs)))
  bound = bnd_ref[0:1, 0:1]
  is_ctx = pl.program_id(1) < ctq
  lat_chunks = [(ct + j * KEY_CHUNK, KEY_CHUNK) for j in range(n_chunks)] + [(0, ct)]

  def scores(start, nk):
    k = kv_ref[start:start + nk, 0:LANES]
    return [_dot_nt(k, qm) for qm in qms]

  def v_t(start, nk):
    return kv_ref[pl.ds(start, nk), LANES:2 * LANES].astype(F32).T.astype(BF16)

  def consume(scs, start, nk):
    vt = v_t(start, nk)
    ones = jnp.ones((BF16_ROWS, nk), BF16)
    lhs = [jnp.concatenate([vt[hf * HEAD_DIM:(hf + 1) * HEAD_DIM, :], ones], axis=0) for hf in range(2)]
    for h, sc in enumerate(scs):
      p = jnp.exp2(sc - bound).astype(BF16)
      pv = jnp.dot(lhs[h % 2], p, preferred_element_type=F32)
      l_sc[h] = l_sc[h] + pv[HEAD_DIM:HEAD_DIM + 1]
      acc_sc[h] = acc_sc[h] + pv[0:HEAD_DIM]

  def run(chunks):
    scs = scores(*chunks[0])
    for i, ch in enumerate(chunks):
      nxt = scores(*chunks[i + 1]) if i + 1 < len(chunks) else None
      consume(scs, *ch)
      scs = nxt

  l_sc[...] = jnp.zeros(l_sc.shape, F32)
  acc_sc[...] = jnp.zeros(acc_sc.shape, F32)

  @pl.when(is_ctx)
  def _():
    run([(0, ct)])

  @pl.when(jnp.logical_not(is_ctx))
  def _():
    run(lat_chunks)

  @pl.when(jnp.min(l_sc[...]) < UNDERFLOW_GUARD)
  def _():
    m_sc[...] = jnp.full(m_sc.shape, NEG_BIG, F32)
    l_sc[...] = jnp.zeros(l_sc.shape, F32)
    acc_sc[...] = jnp.zeros(acc_sc.shape, F32)

    def online(start, nk):
      k = kv_ref[pl.ds(start, nk), 0:LANES]
      vt = v_t(start, nk)
      for h, qm in enumerate(qms):
        hf = h % 2
        sc = _dot_nt(k, qm)
        m_prev = m_sc[h]
        m_new = jnp.maximum(m_prev, sc.max(axis=0, keepdims=True))
        alpha = jnp.exp2(m_prev - m_new)
        p = jnp.exp2(sc - m_new)
        l_sc[h] = alpha * l_sc[h] + p.sum(axis=0, keepdims=True)
        pv = jnp.dot(vt[hf * HEAD_DIM:(hf + 1) * HEAD_DIM, :], p.astype(BF16),
                     preferred_element_type=F32)
        acc_sc[h] = alpha * acc_sc[h] + pv
        m_sc[h] = m_new

    def body(j, carry):
      online(pl.multiple_of(ct + j * KEY_CHUNK, ct), KEY_CHUNK)
      return carry

    lax.fori_loop(0, jnp.where(is_ctx, 0, n_chunks), body, 0)
    online(0, ct)

  for s in range(2):
    ot = jnp.concatenate([acc_sc[2 * s] / l_sc[2 * s], acc_sc[2 * s + 1] / l_sc[2 * s + 1]], axis=0)
    o_ref[:, s * LANES:(s + 1) * LANES] = ot.T.astype(BF16)


def _score_bound(qk_g):
  gq = jnp.max(jnp.abs(qk_g[0]))
  gk = jnp.max(jnp.abs(qk_g[1]))
  bound = BOUND_MARGIN * HEAD_DIM * gq * gk * (HEAD_DIM ** -0.5 * LOG2E)
  return jnp.broadcast_to(bound, (MOD_ROWS, LANES)).astype(F32)


def _glob_attn(qb, kvb, bound, dims):
  b, n, ct = dims
  tq = Q_TILE
  sa = n + ct
  return pl.pallas_call(
      functools.partial(_glob_attn_kernel, ct=ct, ctq=ct // tq, n_chunks=n // KEY_CHUNK),
      grid=(b, sa // tq),
      in_specs=[
          pl.BlockSpec((tq, GROUP_W), lambda bi, j: (bi * (sa // tq) + j, 0)),
          pl.BlockSpec((sa, GROUP_W), lambda bi, j: (bi, 0)),
          pl.BlockSpec((MOD_ROWS, LANES), lambda bi, j: (0, 0)),
      ],
      out_specs=pl.BlockSpec((tq, GROUP_W), lambda bi, j: (bi * (sa // tq) + j, 0)),
      out_shape=jax.ShapeDtypeStruct((b * sa, GROUP_W), BF16),
      scratch_shapes=[pltpu.VMEM((4, 1, tq), F32), pltpu.VMEM((4, 1, tq), F32),
                      pltpu.VMEM((4, HEAD_DIM, tq), F32)],
      compiler_params=pltpu.CompilerParams(
          dimension_semantics=("arbitrary", "arbitrary"), vmem_limit_bytes=VMEM_LIMIT),
      name="global_attn",
  )(qb, kvb, bound)


def _ret_kernel(rf_ref, rb_ref, gf_ref, gb_ref, dec_ref, lw_ref, ng_ref, of_ref, ob_ref,
                sf_sc, sb_sc):
  j = pl.program_id(1)

  @pl.when(j == 0)
  def _():
    sf_sc[...] = jnp.zeros(sf_sc.shape, F32)
    sb_sc[...] = jnp.zeros(sb_sc.shape, F32)

  lo = _lane_lo((1, LANES))
  rr = lax.broadcasted_iota(jnp.int32, (LANES, LANES), 0) // HEAD_DIM
  cc = lax.broadcasted_iota(jnp.int32, (LANES, LANES), 1) // HEAD_DIM
  blockdiag = rr == cc
  ng = ng_ref[...]

  for d, (r_ref, g_ref, o_ref, s_sc) in enumerate(
      ((rf_ref, gf_ref, of_ref, sf_sc), (rb_ref, gb_ref, ob_ref, sb_sc))):
    for s in range(2):
      sl = slice(s * LANES, (s + 1) * LANES)
      q = r_ref[:, s * LANES:(s + 1) * LANES]
      k = r_ref[:, 256 + s * LANES:256 + (s + 1) * LANES]
      v = r_ref[:, 512 + s * LANES:512 + (s + 1) * LANES]
      qw = lw_ref[d, 0][:, sl]
      kw = lw_ref[d, 1][:, sl]
      cd = lw_ref[d, 2][0:1, sl]
      st = s_sc[s]
      o = _dot(q.astype(F32) * qw, st)
      halves = []
      for hf in range(2):
        qm = jnp.where(lo if hf == 0 else jnp.logical_not(lo), q, jnp.zeros_like(q))
        att = _dot_nt(qm, k) * dec_ref[d, 2 * s + hf]
        halves.append(_dot(att, v))
      o = o + jnp.where(lo, halves[0], halves[1])
      u = _dot_tn(k.astype(F32) * kw, v)
      s_sc[s] = st * cd + jnp.where(blockdiag, u, 0.0)
      gate = g_ref[:, d * GROUP_W + s * LANES:d * GROUP_W + (s + 1) * LANES]
      o_ref[:, sl] = _head_norm_slab(o, lo) * ng[0:1, sl] * _silu(gate)


def _scan_maps(bpb, ncx):
  def fwd(j):
    return j

  def bwd(j):
    return jnp.where(j < ncx, ncx - 1 - j, bpb - 1 - (j - ncx))

  return fwd, bwd


def _retention(rc, gc, dec, lw, ng, dims):
  b, n, ct = dims
  c = RET_CHUNK
  rows = rc.shape[0]
  bpb = (n + ct) // c
  f, r = _scan_maps(bpb, ct // c)
  fwd = lambda bi, j: (bi * bpb + f(j), 0)
  bwd = lambda bi, j: (bi * bpb + r(j), 0)

  return pl.pallas_call(
      _ret_kernel,
      grid=(b, bpb),
      in_specs=[
          pl.BlockSpec((c, 768), fwd),
          pl.BlockSpec((c, 768), bwd),
          pl.BlockSpec((c, 512), fwd),
          pl.BlockSpec((c, 512), bwd),
          pl.BlockSpec((2, 4, c, c), lambda bi, j: (0, 0, 0, 0)),
          pl.BlockSpec((2, 3, c, GROUP_W), lambda bi, j: (0, 0, 0, 0)),
          pl.BlockSpec((1, GROUP_W), lambda bi, j: (0, 0)),
      ],
      out_specs=[pl.BlockSpec((c, GROUP_W), fwd), pl.BlockSpec((c, GROUP_W), bwd)],
      out_shape=[jax.ShapeDtypeStruct((rows, GROUP_W), F32)] * 2,
      scratch_shapes=[pltpu.VMEM((2, LANES, LANES), F32)] * 2,
      compiler_params=pltpu.CompilerParams(
          dimension_semantics=("arbitrary", "arbitrary"), vmem_limit_bytes=VMEM_LIMIT),
      name="retention",
  )(rc, rc, gc, gc, dec, lw, ng)


def _rwkv_kernel(df_ref, db_ref, mu_ref, w0_ref, w2_ref, a0_ref, a2_ref, rho_ref, kk_ref, ka_ref,
                 lng_ref, lnb_ref, yf_ref, yb_ref, s_sc, z_sc, *, ncx, nbatch):
  j = pl.program_id(0)
  c = RWKV_CHUNK
  sr = N_HEADS * c
  at_start = jnp.logical_or(j == 0, j == ncx)

  @pl.when(j == 0)
  def _():
    s_sc[...] = jnp.zeros(s_sc.shape, F32)

  @pl.when(at_start)
  def _():
    z_sc[:, :, 0:8, :] = jnp.zeros((2, nbatch, 8, z_sc.shape[3]), F32)
    z_sc[:, :, c + 8:c + 16, :] = jnp.zeros((2, nbatch, 8, z_sc.shape[3]), F32)

  lane_head = lax.broadcasted_iota(jnp.int32, (1, GROUP_W), 1) // HEAD_DIM
  head_masks = [lane_head == h for h in range(N_HEADS)]
  rr = lax.broadcasted_iota(jnp.int32, (GROUP_W, GROUP_W), 0) // HEAD_DIM
  cc = lax.broadcasted_iota(jnp.int32, (GROUP_W, GROUP_W), 1) // HEAD_DIM
  blockdiag = rr == cc
  ti = lax.broadcasted_iota(jnp.int32, (c, c), 0)
  si = lax.broadcasted_iota(jnp.int32, (c, c), 1)
  ts = lax.broadcasted_iota(jnp.int32, (sr, sr), 0)
  ss = lax.broadcasted_iota(jnp.int32, (sr, sr), 1)
  eye = jnp.where(ts == ss, 1.0, 0.0)
  ts, ss = ts % c, ss % c
  kkp = kk_ref[...]
  kap = ka_ref[...]
  lng = lng_ref[...]
  lnb = lnb_ref[...]
  lo = _lane_lo((1, LANES))

  def stack(x):
    return jnp.concatenate([jnp.where(m, x, 0.0) for m in head_masks], axis=0)

  def unstack(x):
    return x[0:c] + x[c:2 * c] + x[2 * c:3 * c] + x[3 * c:4 * c]

  st = {}

  def stage0_mm(ch):
    d, bi = ch
    d_ref = df_ref if d == 0 else db_ref
    upto_c = (si <= ti) if d == 0 else (si >= ti)
    z = jnp.concatenate([d_ref[bi, :, 0:768], d_ref[bi, :, 896 + d * LANES:1024 + d * LANES]], axis=-1)
    z_sc[d, bi, 8:c + 8, :] = z
    zs = z_sc[d, bi, 7:c + 7, :] if d == 0 else z_sc[d, bi, 9:c + 9, :]
    zm = z + mu_ref[d:d + 1, :] * (zs - z)
    if d == 0:
      z_sc[d, bi, 7:8, :] = z[c - 1:c, :]
    else:
      z_sc[d, bi, c + 8:c + 9, :] = z[0:1, :]
    wa = zm[:, 768:896]
    logw = -DECAY_SCALE * jax.nn.sigmoid(w0_ref[d:d + 1, :] + _dot(jnp.tanh(wa), w2_ref[d]))
    ag = jax.nn.sigmoid(a0_ref[d:d + 1, :] + _dot(wa, a2_ref[d]))
    tri = jnp.where(upto_c, 1.0, 0.0).astype(BF16)
    cum = sum(jnp.dot(tri, piece, preferred_element_type=F32) for piece in _split3(logw))
    st[ch] = dict(zm=zm, logw=logw, ag=ag, cum=cum)

  def stage0_rest(ch):
    d, bi = ch
    e = st[ch]
    zm, logw, ag, cum = e["zm"], e["logw"], e["ag"], e["cum"]
    r = zm[:, 0:256]
    k = zm[:, 256:512]
    v = zm[:, 512:768]
    kkr = k * kkp
    kkn = kkr * jnp.minimum(lax.rsqrt(_head_sum(kkr * kkr)), 1e12)
    kt = k * (1.0 + (ag - 1.0) * kap)
    bonus = _head_sum(r * kt * rho_ref[d:d + 1, :]) * v
    tot = cum[c - 1:c, :] if d == 0 else cum[0:1, :]
    e_neg = jnp.exp(-cum)
    e_rest = jnp.exp(tot - cum)
    bvec = kkn * ag
    at = -kkn * jnp.exp(cum - logw)
    rt = r * jnp.exp(cum)
    st[ch] = dict(
        v=v, bonus=bonus, g_all=jnp.exp(tot), at=at, rt=rt,
        x=jnp.concatenate([stack(at), stack(rt)], axis=0),
        y=jnp.concatenate([stack(bvec * e_neg), stack(kt * e_neg)], axis=0),
        bk=jnp.concatenate([bvec * e_rest, kt * e_rest], axis=0),
        vm=stack(v))

  def stage1(ch):
    d, bi = ch
    e = st[ch]
    before = (ss < ts) if d == 0 else (ss > ts)
    upto = (ss <= ts) if d == 0 else (ss >= ts)
    a_all = _dot_nt(e["x"], e["y"])
    e["lab"] = jnp.where(before, a_all[0:sr, 0:sr], 0.0)
    lak = jnp.where(before, a_all[0:sr, sr:2 * sr], 0.0)
    e["p"] = jnp.concatenate([jnp.where(upto, a_all[sr:2 * sr, 0:sr], 0.0),
                              jnp.where(upto, a_all[sr:2 * sr, sr:2 * sr], 0.0)], axis=-1)
    s0 = s_sc[d, bi]
    e["s0"] = s0
    ar = _dot_nt(jnp.concatenate([e["at"], e["rt"]], axis=0), s0)
    e["rh"] = ar[c:2 * c]
    e["xs"] = stack(ar[0:c]) + _dot(lak, e["vm"])

  def stage2(group):
    for ch in group:
      e = st[ch]
      e["lp"] = _dot(e["lab"], e["lab"])
      e["t"] = eye + e["lab"]
    for it in range(4):
      for ch in group:
        e = st[ch]
        lp = e["lp"]
        e["t"] = e["t"] + _dot(e["t"], lp)
        if it < 3:
          e["lp"] = _dot(lp, lp)

  def stage3(group):
    for ch in group:
      e = st[ch]
      e["us"] = _dot(e["t"], e["xs"])
    for ch in group:
      d, bi = ch
      e = st[ch]
      ys = _dot(e["p"], jnp.concatenate([e["us"], e["vm"]], axis=0))
      y = e["rh"] + unstack(ys)
      u = unstack(e["us"])
      upd = _dot_tn(jnp.concatenate([u, e["v"]], axis=0), e["bk"])
      s_sc[d, bi] = e["s0"] * e["g_all"] + jnp.where(blockdiag, upd, 0.0)
      y_ref = yf_ref if d == 0 else yb_ref
      for s in range(2):
        sl = slice(s * LANES, (s + 1) * LANES)
        y_ref[bi, :, sl] = (_head_norm_slab(y[:, sl], lo) * lng[0:1, sl] + lnb[0:1, sl]
                            + e["bonus"][:, sl])

  chains = [(d, bi) for d in range(2) for bi in range(nbatch)]
  for ch in chains:
    stage0_mm(ch)
  for ch in chains:
    stage0_rest(ch)
  for ch in chains:
    stage1(ch)
  stage2(chains)
  stage3(chains)


def _rwkv(dd, mu, w0, w2p, a0, a2p, rho, kk, ka, lng, lnb, dims):
  b, n, ct = dims
  c = RWKV_CHUNK
  sa = n + ct
  bpb = sa // c
  ncx = ct // c
  zw = 768 + LANES
  f, r = _scan_maps(bpb, ncx)
  fwd = lambda j: (0, f(j), 0)
  bwd = lambda j: (0, r(j), 0)
  const2 = lambda j: (0, 0)
  const3 = lambda j: (0, 0, 0)
  d3 = dd.reshape(b, sa, D_COLS)

  yf, yb = pl.pallas_call(
      functools.partial(_rwkv_kernel, ncx=ncx, nbatch=b),
      grid=(bpb,),
      in_specs=[
          pl.BlockSpec((b, c, D_COLS), fwd),
          pl.BlockSpec((b, c, D_COLS), bwd),
          pl.BlockSpec((2, zw), const2),
          pl.BlockSpec((2, GROUP_W), const2),
          pl.BlockSpec((2, LANES, GROUP_W), const3),
          pl.BlockSpec((2, GROUP_W), const2),
          pl.BlockSpec((2, LANES, GROUP_W), const3),
          pl.BlockSpec((2, GROUP_W), const2),
          pl.BlockSpec((1, GROUP_W), const2),
          pl.BlockSpec((1, GROUP_W), const2),
          pl.BlockSpec((1, GROUP_W), const2),
          pl.BlockSpec((1, GROUP_W), const2),
      ],
      out_specs=[pl.BlockSpec((b, c, GROUP_W), fwd), pl.BlockSpec((b, c, GROUP_W), bwd)],
      out_shape=[jax.ShapeDtypeStruct((b, sa, GROUP_W), F32)] * 2,
      scratch_shapes=[pltpu.VMEM((2, b, GROUP_W, GROUP_W), F32), pltpu.VMEM((2, b, c + 16, zw), F32)],
      compiler_params=pltpu.CompilerParams(
          dimension_semantics=("arbitrary",), vmem_limit_bytes=VMEM_LIMIT),
      name="rwkv7",
  )(d3, d3, mu, w0, w2p, a0, a2p, rho, kk, ka, lng, lnb)
  return yf.reshape(b * sa, GROUP_W), yb.reshape(b * sa, GROUP_W)


_Q_HEAD_ORDER = (0, 2, 1, 3)


def _q_perm():
  return np.concatenate([np.arange(h * HEAD_DIM, (h + 1) * HEAD_DIM) for h in _Q_HEAD_ORDER])


def _proj_col_perm():
  qp = _q_perm()
  cols = [qp, np.arange(256, 512), 512 + qp, np.arange(768, 1024), np.arange(1024, 2304)]
  d0 = 2304
  cols.append(np.arange(d0, d0 + 768 + 128))
  wdf, wdb, adf, adb = (np.arange(d0 + 896 + 64 * t, d0 + 896 + 64 * (t + 1)) for t in range(4))
  cols += [wdf, adf, wdb, adb]
  return np.concatenate(cols)


def _rope_tables(n, tail):
  def cos_sin(pos, dim):
    inv = 1.0 / (ROPE_BASE ** (jnp.arange(0, dim, 2, dtype=F32) / dim))
    ang = pos.astype(F32)[:, None] * inv[None, :]
    return jnp.cos(ang), jnp.sin(ang)

  rows = n // GRID_W
  row = jnp.repeat(jnp.arange(rows), GRID_W)
  col = jnp.arange(rows * GRID_W) % GRID_W
  cr, sr = cos_sin(row, HEAD_DIM // 2)
  cc, sc = cos_sin(col, HEAD_DIM // 2)
  cq, sq = cos_sin(jnp.arange(n), HEAD_DIM)
  cax = jnp.concatenate([cr, cr, cc, cc] * 2, axis=-1)
  sax = jnp.concatenate([-sr, sr, -sc, sc] * 2, axis=-1)
  csq = jnp.concatenate([cq, cq] * 2, axis=-1)
  ssq = jnp.concatenate([-sq, sq] * 2, axis=-1)
  ones = jnp.ones((tail, LANES), F32)
  zeros = jnp.zeros((tail, LANES), F32)
  return (jnp.concatenate([cax, ones]), jnp.concatenate([sax, zeros]),
          jnp.concatenate([csq, ones]), jnp.concatenate([ssq, zeros]))


def _retention_tables():
  c = RET_CHUNK
  lg = jnp.log1p(-jnp.exp2(-5.0 - jnp.arange(4, dtype=F32)))
  idx = jnp.arange(c, dtype=F32)
  rel = idx[:, None] - idx[None, :]
  dfw = jnp.where(rel[None] >= 0, jnp.exp(jnp.maximum(rel, 0.0)[None] * lg[:, None, None]), 0.0)
  dec = jnp.stack([dfw, jnp.swapaxes(dfw, 1, 2)])
  lane_lg = jnp.repeat(lg, HEAD_DIM)[None, :]
  qw_f = jnp.exp((idx + 1.0)[:, None] * lane_lg)
  kw_f = jnp.exp((c - 1.0 - idx)[:, None] * lane_lg)
  qw_b = jnp.exp((c - idx)[:, None] * lane_lg)
  kw_b = jnp.exp(idx[:, None] * lane_lg)
  cd = jnp.broadcast_to(jnp.exp(c * lane_lg), (c, GROUP_W))
  lw = jnp.stack([jnp.stack([qw_f, kw_f, cd]), jnp.stack([qw_b, kw_b, cd])])
  return dec, lw


def kernel(x, c, ctx, c_ctx, w_mod, b_mod, norm_g, ffn_w_in, ffn_w_out, w_in, w_out, attn_sink,
           qk_norm_g, ret_norm_g, rwkv_mu, rwkv_w0, rwkv_w2, rwkv_a0, rwkv_a2, rwkv_rho, rwkv_k_k,
           rwkv_k_a, rwkv_g2, rwkv_ln_g, rwkv_ln_b, final_norm_g):
  b, n, d = x.shape
  ct = ctx.shape[1]
  depth = w_mod.shape[0]
  assert d == D_MODEL and b < MOD_ROWS
  assert n % ROW_TILE == 0 and ct % ROW_TILE == 0 and n % KEY_CHUNK == 0 and ct % Q_TILE == 0
  assert n % RET_CHUNK == 0 and ct % RET_CHUNK == 0 and n % GRID_W == 0 and KEY_CHUNK % ct == 0
  dims = (b, n, ct)

  cs = jnp.zeros((MOD_ROWS, d), F32).at[:b].set(c).at[b].set(c_ctx)
  mod = _modulation(cs, w_mod, b_mod).reshape(depth, MOD_ROWS, N_MOD, d)

  tabs = _rope_tables(n, ROW_TILE)
  dec, lw = _retention_tables()
  col_perm = _proj_col_perm()
  q_perm = _q_perm()
  out_perm = np.concatenate([q_perm, 256 + q_perm, np.arange(512, 1024)])
  zpad = jnp.zeros((2, A_RANK, GROUP_W), F32)

  xs = jnp.concatenate([ctx, x], axis=1).reshape(b * (ct + n), d)
  for l in range(depth):
    last = l == depth - 1
    w1a, w2a = ffn_w_in[l, 0].astype(BF16), ffn_w_out[l, 0].astype(BF16)
    w1b, w2b = ffn_w_in[l, 1].astype(BF16), ffn_w_out[l, 1].astype(BF16)
    win = w_in[l][:, col_perm].astype(BF16)
    wo = w_out[l][out_perm, :].astype(BF16)
    qkg = jnp.tile(qk_norm_g[l], (1, GROUP_W // HEAD_DIM))
    sink_tab = jnp.broadcast_to((attn_sink[l] * LOG2E)[jnp.asarray(_Q_HEAD_ORDER)][:, None], (4, LANES))
    w2p = jnp.concatenate([rwkv_w2[l], zpad], axis=1)
    a2p = jnp.concatenate([zpad, rwkv_a2[l]], axis=1)

    xs, qa, kva, qb, kvb, rc, gc, dd = _ffn_proj(
        xs, mod[l], norm_g[l], w1a, w2a, win, qkg, tabs, dims)
    oa = _win_attn(qa, kva, sink_tab, dims)
    ob = _glob_attn(qb, kvb, _score_bound(qk_norm_g[l]), dims)
    cf, cb = _retention(rc, gc, dec, lw, ret_norm_g[l][None, :], dims)
    yf, yb = _rwkv(dd, rwkv_mu[l], rwkv_w0[l], w2p, rwkv_a0[l], a2p,
                   rwkv_rho[l].reshape(2, GROUP_W), rwkv_k_k[l][None, :], rwkv_k_a[l][None, :],
                   rwkv_ln_g[l][None, :], rwkv_ln_b[l][None, :], dims)
    xs = _out_ffn(xs, mod[l], norm_g[l], oa, ob, cf, cb, yf, yb, dd, rwkv_g2[l].astype(BF16), wo,
                  w1b, w2b, final_norm_g[None, :], dims, final=last)
  return xs.reshape(b, n, d)
```

```python
import functools

import numpy as np
import jax
import jax.numpy as jnp
from jax import lax
from jax.experimental import pallas as pl
from jax.experimental.pallas import tpu as pltpu

F32 = jnp.float32
BF16 = jnp.bfloat16

D_MODEL = 1024
HEAD_DIM = 64
N_HEADS = 4
GROUP_W = 256
KV_W = 128
GRID_W = 64
BLOCK = 128
D_FF = 2816
A_RANK = 64
G_RANK = 128
N_MOD = 9
ROPE_BASE = 10000.0
RMS_EPS = 1e-6
GN_EPS = 64e-5
DECAY_SCALE = 0.6065306597126334
PROJ_COLS = 3456
D_COLS = 1152
LANES = 128
MOD_ROWS = 8
BF16_ROWS = 16

ROW_TILE = 256
FF_CHUNK = 1408
Q_TILE = 256
WIN_Q = 2 * BLOCK
KEY_CHUNK = 1024
RET_CHUNK = 256
RWKV_CHUNK = 32
VMEM_LIMIT = 56 * 1024 * 1024

NEG_BIG = -1e30
LOG2E = 1.4426950408889634
BOUND_MARGIN = 1.02
UNDERFLOW_GUARD = 2.0 ** -100
HIGHEST = lax.Precision.HIGHEST


def _dot(a, b):
  return jnp.dot(a.astype(BF16), b.astype(BF16), preferred_element_type=F32)


def _dot_nt(a, b):
  return lax.dot_general(a.astype(BF16), b.astype(BF16), (((1,), (1,)), ((), ())),
                         preferred_element_type=F32)


def _dot_tn(a, b):
  return jnp.dot(a.astype(F32).T.astype(BF16), b.astype(BF16), preferred_element_type=F32)


def _silu(x):
  return x * jax.nn.sigmoid(x)


def _lane_lo(shape):
  return (lax.broadcasted_iota(jnp.int32, shape, len(shape) - 1) % LANES) < HEAD_DIM


def _half_sum(x, lo):
  s_lo = jnp.sum(jnp.where(lo, x, 0.0), axis=-1, keepdims=True)
  s_all = jnp.sum(x, axis=-1, keepdims=True)
  return jnp.where(lo, s_lo, s_all - s_lo)


def _head_sum(x):
  lo = _lane_lo((1, LANES))
  return jnp.concatenate([_half_sum(x[:, o:o + LANES], lo) for o in range(0, GROUP_W, LANES)], axis=-1)


def _head_norm_slab(y, lo):
  mu = _half_sum(y, lo) * (1.0 / HEAD_DIM)
  d = y - mu
  var = _half_sum(d * d, lo) * (1.0 / HEAD_DIM)
  return d * lax.rsqrt(var + GN_EPS)


def _rope_slab(x, c, s, half):
  first = (lax.broadcasted_iota(jnp.int32, x.shape, 1) % (2 * half)) < half
  partner = jnp.where(first, pltpu.roll(x, LANES - half, axis=1), pltpu.roll(x, half, axis=1))
  return x * c + partner * s


def _mod_kernel(c_ref, w_ref, b_ref, o_ref):
  s = _silu(c_ref[...])
  o_ref[...] = jnp.dot(s, w_ref[...], precision=HIGHEST, preferred_element_type=F32) + b_ref[...]


def _modulation(cs, w_mod, b_mod):
  depth = w_mod.shape[0]
  return pl.pallas_call(
      _mod_kernel,
      grid=(depth, N_MOD),
      in_specs=[
          pl.BlockSpec((MOD_ROWS, D_MODEL), lambda l, j: (0, 0)),
          pl.BlockSpec((None, D_MODEL, D_MODEL), lambda l, j: (l, 0, j)),
          pl.BlockSpec((None, 1, D_MODEL), lambda l, j: (l, 0, j)),
      ],
      out_specs=pl.BlockSpec((None, MOD_ROWS, D_MODEL), lambda l, j: (l, 0, j)),
      out_shape=jax.ShapeDtypeStruct((depth, MOD_ROWS, N_MOD * D_MODEL), F32),
      compiler_params=pltpu.CompilerParams(
          dimension_semantics=("arbitrary", "arbitrary"), vmem_limit_bytes=VMEM_LIMIT),
      name="modulation",
  )(cs, w_mod, b_mod.reshape(depth, 1, N_MOD * D_MODEL))


def _rms_mod(x, g, shift, scale):
  h = x * lax.rsqrt(jnp.mean(x * x, axis=-1, keepdims=True) + RMS_EPS) * g
  return h * (1.0 + scale) + shift


def _swiglu(h, w1_ref, w2_ref):
  hb = h.astype(BF16)
  acc = None
  for c0 in range(0, D_FF, FF_CHUNK):
    u1 = jnp.dot(hb, w1_ref[:, c0:c0 + FF_CHUNK], preferred_element_type=F32)
    u2 = jnp.dot(hb, w1_ref[:, D_FF + c0:D_FF + c0 + FF_CHUNK], preferred_element_type=F32)
    a = (_silu(u1) * u2).astype(BF16)
    part = jnp.dot(a, w2_ref[c0:c0 + FF_CHUNK, :], preferred_element_type=F32)
    acc = part if acc is None else acc + part
  return acc


def _head_rms(x, g):
  lo = _lane_lo((1, LANES))
  sq = x * x
  ms = jnp.concatenate([_half_sum(sq[:, o:o + LANES], lo) for o in range(0, x.shape[-1], LANES)],
                       axis=-1) * (1.0 / HEAD_DIM)
  return x * lax.rsqrt(ms + RMS_EPS) * g


def _split3(x):
  hi = x.astype(BF16)
  r1 = x - hi.astype(F32)
  mid = r1.astype(BF16)
  lo = (r1 - mid.astype(F32)).astype(BF16)
  return hi, mid, lo


def _rope_wide(x, c, s, half):
  parts = [_rope_slab(x[:, o:o + LANES], c, s, half) for o in range(0, x.shape[-1], LANES)]
  return parts[0] if len(parts) == 1 else jnp.concatenate(parts, axis=-1)


def _ffn_proj_kernel(*refs, tpb, ctt, split):
  if split:
    x_ref, xc_ref, *refs = refs
    x = jnp.where(pl.program_id(0) % tpb < ctt, xc_ref[...], x_ref[...])
  else:
    x_ref, *refs = refs
    x = x_ref[...]
  (mod_ref, ng_ref, w1_ref, w2_ref, win_ref, qkg_ref, cax_ref, sax_ref, csq_ref, ssq_ref,
   xo_ref, qa_ref, kva_ref, qb_ref, kvb_ref, rc_ref, gc_ref, dd_ref) = refs
  m = mod_ref[...]
  ng = ng_ref[...]
  h = _rms_mod(x, ng[0:1], m[0:1], m[1:2])
  x1 = x + 0.5 * m[2:3] * _swiglu(h, w1_ref, w2_ref)
  xo_ref[...] = x1
  hb = _rms_mod(x1, ng[1:2], m[3:4], m[4:5]).astype(BF16)

  cax, sax = cax_ref[...], sax_ref[...]
  csq, ssq = csq_ref[...], ssq_ref[...]
  qkg = qkg_ref[...]
  scale = HEAD_DIM ** -0.5

  p = jnp.dot(hb, win_ref[:, 0:512], preferred_element_type=F32)
  qa_ref[...] = (_rope_wide(p[:, 0:256], cax, sax, 16) * (scale * LOG2E)).astype(BF16)
  kva_ref[:, 0:128] = _rope_wide(p[:, 256:384], cax, sax, 16).astype(BF16)
  kva_ref[:, 128:256] = p[:, 384:512].astype(BF16)
  p = jnp.dot(hb, win_ref[:, 512:1024], preferred_element_type=F32)
  qn = _head_rms(p[:, 0:256], qkg[0:1, :])
  kn = _head_rms(p[:, 256:384], qkg[1:2, 0:128])
  qb_ref[...] = (_rope_wide(qn, cax, sax, 16) * (scale * LOG2E)).astype(BF16)
  kvb_ref[:, 0:128] = _rope_wide(kn, cax, sax, 16).astype(BF16)
  kvb_ref[:, 128:256] = p[:, 384:512].astype(BF16)
  p = jnp.dot(hb, win_ref[:, 1024:2304], preferred_element_type=F32)
  rc_ref[:, 0:256] = _rope_wide(p[:, 0:256], csq, ssq, 32).astype(BF16)
  rc_ref[:, 256:512] = (_rope_wide(p[:, 256:512], csq, ssq, 32) * scale).astype(BF16)
  rc_ref[:, 512:768] = p[:, 512:768].astype(BF16)
  gc_ref[...] = p[:, 768:1280]
  dd_ref[...] = jnp.dot(hb, win_ref[:, 2304:3456], preferred_element_type=F32)


def _ffn_proj(xs, mod, ng, w1, w2, win, qkg, tabs, dims):
  b, n, ct = dims
  rows = b * (n + ct)
  tm = ROW_TILE
  tpb = (n + ct) // tm
  ctt = ct // tm
  ident = n // tm

  def row(i):
    return (i, 0)

  def const(i):
    return (0, 0)

  def mod_idx(i):
    return (jnp.where(i % tpb < ctt, b, i // tpb), 0, 0)

  def tab_idx(i):
    t = i % tpb
    return (jnp.where(t < ctt, ident, t - ctt), 0)

  split = isinstance(xs, tuple)
  if split:
    def lat_row(i):
      return ((i // tpb) * (n // tm) + jnp.maximum(i % tpb - ctt, 0), 0)

    def ctx_row(i):
      return ((i // tpb) * ctt + jnp.minimum(i % tpb, ctt - 1), 0)

    x_specs = [pl.BlockSpec((tm, D_MODEL), lat_row), pl.BlockSpec((tm, D_MODEL), ctx_row)]
    xs = list(xs)
  else:
    x_specs = [pl.BlockSpec((tm, D_MODEL), row)]
    xs = [xs]

  widths = (D_MODEL, 256, 256, 256, 256, 768, 512, D_COLS)
  dtypes = (F32, BF16, BF16, BF16, BF16, BF16, F32, F32)
  return pl.pallas_call(
      functools.partial(_ffn_proj_kernel, tpb=tpb, ctt=ctt, split=split),
      grid=(rows // tm,),
      in_specs=x_specs + [
          pl.BlockSpec((None, N_MOD, D_MODEL), mod_idx),
          pl.BlockSpec((3, D_MODEL), const),
          pl.BlockSpec((D_MODEL, 2 * D_FF), const),
          pl.BlockSpec((D_FF, D_MODEL), const),
          pl.BlockSpec((D_MODEL, PROJ_COLS), const),
          pl.BlockSpec((2, GROUP_W), const),
          pl.BlockSpec((tm, LANES), tab_idx),
          pl.BlockSpec((tm, LANES), tab_idx),
          pl.BlockSpec((tm, LANES), tab_idx),
          pl.BlockSpec((tm, LANES), tab_idx),
      ],
      out_specs=[pl.BlockSpec((tm, w), row) for w in widths],
      out_shape=[jax.ShapeDtypeStruct((rows, w), dt) for w, dt in zip(widths, dtypes)],
      compiler_params=pltpu.CompilerParams(
          dimension_semantics=("arbitrary",), vmem_limit_bytes=VMEM_LIMIT),
      name="ffn_proj",
  )(*xs, mod, ng, w1, w2, win, qkg, *tabs)


def _out_ffn_kernel(x_ref, mod_ref, ng_ref, oa_ref, ob_ref, cf_ref, cb_ref, yf_ref, yb_ref,
                    gd_ref, g2_ref, wo_ref, w1_ref, w2_ref, fg_ref, xo_ref, *, final):
  x = x_ref[...]
  m = mod_ref[...]
  ng = ng_ref[...]
  gate = _dot(jax.nn.sigmoid(gd_ref[...]), g2_ref[...])
  od = (yf_ref[...] + yb_ref[...]) * gate
  oc = cf_ref[...] + cb_ref[...]
  y = jnp.dot(oa_ref[...], wo_ref[0:256, :], preferred_element_type=F32)
  y = y + jnp.dot(ob_ref[...], wo_ref[256:512, :], preferred_element_type=F32)
  y = y + jnp.dot(oc.astype(BF16), wo_ref[512:768, :], preferred_element_type=F32)
  y = y + jnp.dot(od.astype(BF16), wo_ref[768:1024, :], preferred_element_type=F32)
  x1 = x + m[5:6] * y
  h = _rms_mod(x1, ng[2:3], m[6:7], m[7:8])
  x2 = x1 + 0.5 * m[8:9] * _swiglu(h, w1_ref, w2_ref)
  if final:
    x2 = x2 * lax.rsqrt(jnp.mean(x2 * x2, axis=-1, keepdims=True) + RMS_EPS) * fg_ref[...]
  xo_ref[...] = x2


def _out_ffn(x, mod, ng, oa, ob, cf, cb, yf, yb, dd, g2, wo, w1, w2, fg, dims, final):
  b, n, ct = dims
  tm = ROW_TILE
  tpb = (n + ct) // tm
  ctt = ct // tm
  if final:
    grid = (b, n // tm)
    row = lambda bi, j: (bi * tpb + ctt + j, 0)
    out_row = lambda bi, j: (bi * (n // tm) + j, 0)
    mod_idx = lambda bi, j: (bi, 0, 0)
    gd_idx = lambda bi, j: (bi * tpb + ctt + j, 768 // G_RANK)
    out_rows = b * n
  else:
    grid = (b, tpb)
    row = lambda bi, j: (bi * tpb + j, 0)
    out_row = row
    mod_idx = lambda bi, j: (jnp.where(j < ctt, b, bi), 0, 0)
    gd_idx = lambda bi, j: (bi * tpb + j, 768 // G_RANK)
    out_rows = x.shape[0]
  const = lambda bi, j: (0, 0)

  return pl.pallas_call(
      functools.partial(_out_ffn_kernel, final=final),
      grid=grid,
      in_specs=[
          pl.BlockSpec((tm, D_MODEL), row),
          pl.BlockSpec((None, N_MOD, D_MODEL), mod_idx),
          pl.BlockSpec((3, D_MODEL), const),
          pl.BlockSpec((tm, GROUP_W), row),
          pl.BlockSpec((tm, GROUP_W), row),
          pl.BlockSpec((tm, GROUP_W), row),
          pl.BlockSpec((tm, GROUP_W), row),
          pl.BlockSpec((tm, GROUP_W), row),
          pl.BlockSpec((tm, GROUP_W), row),
          pl.BlockSpec((tm, G_RANK), gd_idx),
          pl.BlockSpec((G_RANK, GROUP_W), const),
          pl.BlockSpec((D_MODEL, D_MODEL), const),
          pl.BlockSpec((D_MODEL, 2 * D_FF), const),
          pl.BlockSpec((D_FF, D_MODEL), const),
          pl.BlockSpec((1, D_MODEL), const),
      ],
      out_specs=pl.BlockSpec((tm, D_MODEL), out_row),
      out_shape=jax.ShapeDtypeStruct((out_rows, D_MODEL), F32),
      compiler_params=pltpu.CompilerParams(
          dimension_semantics=("arbitrary", "arbitrary"), vmem_limit_bytes=VMEM_LIMIT),
      name="out_ffn_final" if final else "out_ffn",
  )(x, mod, ng, oa, ob, cf, cb, yf, yb, dd, g2, wo, w1, w2, fg)


def _win_attn_kernel(q_ref, k0_ref, k1_ref, k2_ref, k3_ref, kx_ref, sink_ref, o_ref, *, spb, cs, nb):
  t = pl.program_id(0) % spb
  is_lat = t >= cs
  n = 2 * (t - cs)
  big = 4 * BLOCK
  off_lat = jnp.where(is_lat, 0, 2 * big)
  off0 = jnp.where(jnp.logical_and(is_lat, n > 0), 0, 2 * big)
  off3 = jnp.where(jnp.logical_and(is_lat, n + 2 < nb), 0, 2 * big)
  r = lax.broadcasted_iota(jnp.int32, (BLOCK, WIN_Q), 0)
  lane = lax.broadcasted_iota(jnp.int32, (BLOCK, WIN_Q), 1)
  d = r - lane % BLOCK
  first = lane < BLOCK
  segs = [
      (k0_ref, d >= jnp.where(first, 0, big) + off0),
      (k1_ref, d >= jnp.where(first, -big, 0) + off_lat),
      (k2_ref, -d >= jnp.where(first, 0, -big) + off_lat),
      (k3_ref, -d >= jnp.where(first, big, 0) + off3),
      (kx_ref, None),
  ]
  lo = _lane_lo((1, LANES))
  qms = []
  for s in range(2):
    qs = q_ref[:, s * LANES:(s + 1) * LANES]
    for hf in range(2):
      qms.append(jnp.where(lo if hf == 0 else jnp.logical_not(lo), qs, jnp.zeros_like(qs)))
  all_scores = []
  for qm in qms:
    scs = []
    for ref, valid in segs:
      sc = _dot_nt(ref[:, 0:LANES], qm)
      scs.append(sc if valid is None else jnp.where(valid, sc, NEG_BIG))
    all_scores.append(scs)
  vts = [ref[:, LANES:2 * LANES].astype(F32).T.astype(BF16) for ref, _ in segs]
  outs = []
  for h, scs in enumerate(all_scores):
    hf = h % 2
    snk = sink_ref[h:h + 1, 0:1]
    mx = jnp.maximum(scs[0].max(axis=0, keepdims=True), snk)
    for sc in scs[1:]:
      mx = jnp.maximum(mx, sc.max(axis=0, keepdims=True))
    den = jnp.exp2(snk - mx)
    acc = None
    for sc, vt in zip(scs, vts):
      p = jnp.exp2(sc - mx)
      den = den + p.sum(axis=0, keepdims=True)
      pv = jnp.dot(vt[hf * HEAD_DIM:(hf + 1) * HEAD_DIM, :], p.astype(BF16),
                   preferred_element_type=F32)
      acc = pv if acc is None else acc + pv
    outs.append(acc / den)
  for s in range(2):
    ot = jnp.concatenate([outs[2 * s], outs[2 * s + 1]], axis=0)
    o_ref[:, s * LANES:(s + 1) * LANES] = ot.T.astype(BF16)


def _win_attn(qa, kva, sink_tab, dims):
  b, n, ct = dims
  rows = qa.shape[0]
  nb = n // BLOCK
  cb = ct // BLOCK
  bpb = nb + cb
  spb = (n + ct) // WIN_Q
  cs = ct // WIN_Q

  def key_block(j):
    def idx(i):
      blk = cb + 2 * (i % spb - cs) + j
      return ((i // spb) * bpb + jnp.clip(blk, cb, bpb - 1), 0)
    return idx

  return pl.pallas_call(
      functools.partial(_win_attn_kernel, spb=spb, cs=cs, nb=nb),
      grid=(rows // WIN_Q,),
      in_specs=[pl.BlockSpec((WIN_Q, GROUP_W), lambda i: (i, 0))]
      + [pl.BlockSpec((BLOCK, GROUP_W), key_block(j)) for j in (-1, 0, 1, 2)]
      + [pl.BlockSpec((ct, GROUP_W), lambda i: ((i // spb) * ((n + ct) // ct), 0)),
         pl.BlockSpec((4, LANES), lambda i: (0, 0))],
      out_specs=pl.BlockSpec((WIN_Q, GROUP_W), lambda i: (i, 0)),
      out_shape=jax.ShapeDtypeStruct((rows, GROUP_W), BF16),
      compiler_params=pltpu.CompilerParams(
          dimension_semantics=("arbitrary",), vmem_limit_bytes=VMEM_LIMIT),
      name="window_attn",
  )(qa, kva, kva, kva, kva, kva, sink_tab)


def _glob_attn_kernel(q_ref, kv_ref, bnd_ref, o_ref, m_sc, l_sc, acc_sc, *, ct, ctq, n_chunks):
  lo = _lane_lo((1, LANES))
  qms = []
  for s in range(2):
    qs = q_ref[:, s * LANES:(s + 1) * LANES]
    for hf in range(2):
      qms.append(jnp.where(lo if hf == 0 else jnp.logical_not(lo), qs, jnp.zeros_like(qs)))
  bound = bnd_ref[0:1, 0:1]
  is_ctx = pl.program_id(1) < ctq
  lat_chunks = [(ct + j * KEY_CHUNK, KEY_CHUNK) for j in range(n_chunks)] + [(0, ct)]

  def scores(start, nk):
    k = kv_ref[start:start + nk, 0:LANES]
    return [_dot_nt(k, qm) for qm in qms]

  def v_t(start, nk):
    return kv_ref[pl.ds(start, nk), LANES:2 * LANES].astype(F32).T.astype(BF16)

  def consume(scs, start, nk):
    vt = v_t(start, nk)
    ones = jnp.ones((BF16_ROWS, nk), BF16)
    lhs = [jnp.concatenate([vt[hf * HEAD_DIM:(hf + 1) * HEAD_DIM, :], ones], axis=0) for hf in range(2)]
    for h, sc in enumerate(scs):
      p = jnp.exp2(sc - bound).astype(BF16)
      pv = jnp.dot(lhs[h % 2], p, preferred_element_type=F32)
      l_sc[h] = l_sc[h] + pv[HEAD_DIM:HEAD_DIM + 1]
      acc_sc[h] = acc_sc[h] + pv[0:HEAD_DIM]

  def run(chunks):
    scs = scores(*chunks[0])
    for i, ch in enumerate(chunks):
      nxt = scores(*chunks[i + 1]) if i + 1 < len(chunks) else None
      consume(scs, *ch)
      scs = nxt

  l_sc[...] = jnp.zeros(l_sc.shape, F32)
  acc_sc[...] = jnp.zeros(acc_sc.shape, F32)

  @pl.when(is_ctx)
  def _():
    run([(0, ct)])

  @pl.when(jnp.logical_not(is_ctx))
  def _():
    run(lat_chunks)

  @pl.when(jnp.min(l_sc[...]) < UNDERFLOW_GUARD)
  def _():
    m_sc[...] = jnp.full(m_sc.shape, NEG_BIG, F32)
    l_sc[...] = jnp.zeros(l_sc.shape, F32)
    acc_sc[...] = jnp.zeros(acc_sc.shape, F32)

    def online(start, nk):
      k = kv_ref[pl.ds(start, nk), 0:LANES]
      vt = v_t(start, nk)
      for h, qm in enumerate(qms):
        hf = h % 2
        sc = _dot_nt(k, qm)
        m_prev = m_sc[h]
        m_new = jnp.maximum(m_prev, sc.max(axis=0, keepdims=True))
        alpha = jnp.exp2(m_prev - m_new)
        p = jnp.exp2(sc - m_new)
        l_sc[h] = alpha * l_sc[h] + p.sum(axis=0, keepdims=True)
        pv = jnp.dot(vt[hf * HEAD_DIM:(hf + 1) * HEAD_DIM, :], p.astype(BF16),
                     preferred_element_type=F32)
        acc_sc[h] = alpha * acc_sc[h] + pv
        m_sc[h] = m_new

    def body(j, carry):
      online(pl.multiple_of(ct + j * KEY_CHUNK, ct), KEY_CHUNK)
      return carry

    lax.fori_loop(0, jnp.where(is_ctx, 0, n_chunks), body, 0)
    online(0, ct)

  for s in range(2):
    ot = jnp.concatenate([acc_sc[2 * s] / l_sc[2 * s], acc_sc[2 * s + 1] / l_sc[2 * s + 1]], axis=0)
    o_ref[:, s * LANES:(s + 1) * LANES] = ot.T.astype(BF16)


def _score_bound(qk_g):
  gq = jnp.max(jnp.abs(qk_g[0]))
  gk = jnp.max(jnp.abs(qk_g[1]))
  bound = BOUND_MARGIN * HEAD_DIM * gq * gk * (HEAD_DIM ** -0.5 * LOG2E)
  return jnp.broadcast_to(bound, (MOD_ROWS, LANES)).astype(F32)


def _glob_attn(qb, kvb, bound, dims):
  b, n, ct = dims
  tq = Q_TILE
  sa = n + ct
  return pl.pallas_call(
      functools.partial(_glob_attn_kernel, ct=ct, ctq=ct // tq, n_chunks=n // KEY_CHUNK),
      grid=(b, sa // tq),
      in_specs=[
          pl.BlockSpec((tq, GROUP_W), lambda bi, j: (bi * (sa // tq) + j, 0)),
          pl.BlockSpec((sa, GROUP_W), lambda bi, j: (bi, 0)),
          pl.BlockSpec((MOD_ROWS, LANES), lambda bi, j: (0, 0)),
      ],
      out_specs=pl.BlockSpec((tq, GROUP_W), lambda bi, j: (bi * (sa // tq) + j, 0)),
      out_shape=jax.ShapeDtypeStruct((b * sa, GROUP_W), BF16),
      scratch_shapes=[pltpu.VMEM((4, 1, tq), F32), pltpu.VMEM((4, 1, tq), F32),
                      pltpu.VMEM((4, HEAD_DIM, tq), F32)],
      compiler_params=pltpu.CompilerParams(
          dimension_semantics=("arbitrary", "arbitrary"), vmem_limit_bytes=VMEM_LIMIT),
      name="global_attn",
  )(qb, kvb, bound)


def _ret_kernel(rf_ref, rb_ref, gf_ref, gb_ref, dec_ref, lw_ref, ng_ref, of_ref, ob_ref,
                sf_sc, sb_sc):
  j = pl.program_id(1)

  @pl.when(j == 0)
  def _():
    sf_sc[...] = jnp.zeros(sf_sc.shape, F32)
    sb_sc[...] = jnp.zeros(sb_sc.shape, F32)

  lo = _lane_lo((1, LANES))
  rr = lax.broadcasted_iota(jnp.int32, (LANES, LANES), 0) // HEAD_DIM
  cc = lax.broadcasted_iota(jnp.int32, (LANES, LANES), 1) // HEAD_DIM
  blockdiag = rr == cc
  ng = ng_ref[...]

  for d, (r_ref, g_ref, o_ref, s_sc) in enumerate(
      ((rf_ref, gf_ref, of_ref, sf_sc), (rb_ref, gb_ref, ob_ref, sb_sc))):
    for s in range(2):
      sl = slice(s * LANES, (s + 1) * LANES)
      q = r_ref[:, s * LANES:(s + 1) * LANES]
      k = r_ref[:, 256 + s * LANES:256 + (s + 1) * LANES]
      v = r_ref[:, 512 + s * LANES:512 + (s + 1) * LANES]
      qw = lw_ref[d, 0][:, sl]
      kw = lw_ref[d, 1][:, sl]
      cd = lw_ref[d, 2][0:1, sl]
      st = s_sc[s]
      o = _dot(q.astype(F32) * qw, st)
      halves = []
      for hf in range(2):
        qm = jnp.where(lo if hf == 0 else jnp.logical_not(lo), q, jnp.zeros_like(q))
        att = _dot_nt(qm, k) * dec_ref[d, 2 * s + hf]
        halves.append(_dot(att, v))
      o = o + jnp.where(lo, halves[0], halves[1])
      u = _dot_tn(k.astype(F32) * kw, v)
      s_sc[s] = st * cd + jnp.where(blockdiag, u, 0.0)
      gate = g_ref[:, d * GROUP_W + s * LANES:d * GROUP_W + (s + 1) * LANES]
      o_ref[:, sl] = _head_norm_slab(o, lo) * ng[0:1, sl] * _silu(gate)


def _scan_maps(bpb, ncx):
  def fwd(j):
    return j

  def bwd(j):
    return jnp.where(j < ncx, ncx - 1 - j, bpb - 1 - (j - ncx))

  return fwd, bwd


def _retention(rc, gc, dec, lw, ng, dims):
  b, n, ct = dims
  c = RET_CHUNK
  rows = rc.shape[0]
  bpb = (n + ct) // c
  f, r = _scan_maps(bpb, ct // c)
  fwd = lambda bi, j: (bi * bpb + f(j), 0)
  bwd = lambda bi, j: (bi * bpb + r(j), 0)

  return pl.pallas_call(
      _ret_kernel,
      grid=(b, bpb),
      in_specs=[
          pl.BlockSpec((c, 768), fwd),
          pl.BlockSpec((c, 768), bwd),
          pl.BlockSpec((c, 512), fwd),
          pl.BlockSpec((c, 512), bwd),
          pl.BlockSpec((2, 4, c, c), lambda bi, j: (0, 0, 0, 0)),
          pl.BlockSpec((2, 3, c, GROUP_W), lambda bi, j: (0, 0, 0, 0)),
          pl.BlockSpec((1, GROUP_W), lambda bi, j: (0, 0)),
      ],
      out_specs=[pl.BlockSpec((c, GROUP_W), fwd), pl.BlockSpec((c, GROUP_W), bwd)],
      out_shape=[jax.ShapeDtypeStruct((rows, GROUP_W), F32)] * 2,
      scratch_shapes=[pltpu.VMEM((2, LANES, LANES), F32)] * 2,
      compiler_params=pltpu.CompilerParams(
          dimension_semantics=("arbitrary", "arbitrary"), vmem_limit_bytes=VMEM_LIMIT),
      name="retention",
  )(rc, rc, gc, gc, dec, lw, ng)


def _rwkv_kernel(df_ref, db_ref, mu_ref, w0_ref, w2_ref, a0_ref, a2_ref, rho_ref, kk_ref, ka_ref,
                 lng_ref, lnb_ref, yf_ref, yb_ref, s_sc, z_sc, *, ncx, nbatch):
  j = pl.program_id(0)
  c = RWKV_CHUNK
  sr = N_HEADS * c
  at_start = jnp.logical_or(j == 0, j == ncx)

  @pl.when(j == 0)
  def _():
    s_sc[...] = jnp.zeros(s_sc.shape, F32)

  @pl.when(at_start)
  def _():
    z_sc[:, :, 0:8, :] = jnp.zeros((2, nbatch, 8, z_sc.shape[3]), F32)
    z_sc[:, :, c + 8:c + 16, :] = jnp.zeros((2, nbatch, 8, z_sc.shape[3]), F32)

  lane_head = lax.broadcasted_iota(jnp.int32, (1, GROUP_W), 1) // HEAD_DIM
  head_masks = [lane_head == h for h in range(N_HEADS)]
  rr = lax.broadcasted_iota(jnp.int32, (GROUP_W, GROUP_W), 0) // HEAD_DIM
  cc = lax.broadcasted_iota(jnp.int32, (GROUP_W, GROUP_W), 1) // HEAD_DIM
  blockdiag = rr == cc
  ti = lax.broadcasted_iota(jnp.int32, (c, c), 0)
  si = lax.broadcasted_iota(jnp.int32, (c, c), 1)
  ts = lax.broadcasted_iota(jnp.int32, (sr, sr), 0)
  ss = lax.broadcasted_iota(jnp.int32, (sr, sr), 1)
  eye = jnp.where(ts == ss, 1.0, 0.0)
  ts, ss = ts % c, ss % c
  kkp = kk_ref[...]
  kap = ka_ref[...]
  lng = lng_ref[...]
  lnb = lnb_ref[...]
  lo = _lane_lo((1, LANES))

  def stack(x):
    return jnp.concatenate([jnp.where(m, x, 0.0) for m in head_masks], axis=0)

  def unstack(x):
    return x[0:c] + x[c:2 * c] + x[2 * c:3 * c] + x[3 * c:4 * c]

  st = {}

  def stage0_mm(ch):
    d, bi = ch
    d_ref = df_ref if d == 0 else db_ref
    upto_c = (si <= ti) if d == 0 else (si >= ti)
    z = jnp.concatenate([d_ref[bi, :, 0:768], d_ref[bi, :, 896 + d * LANES:1024 + d * LANES]], axis=-1)
    z_sc[d, bi, 8:c + 8, :] = z
    zs = z_sc[d, bi, 7:c + 7, :] if d == 0 else z_sc[d, bi, 9:c + 9, :]
    zm = z + mu_ref[d:d + 1, :] * (zs - z)
    if d == 0:
      z_sc[d, bi, 7:8, :] = z[c - 1:c, :]
    else:
      z_sc[d, bi, c + 8:c + 9, :] = z[0:1, :]
    wa = zm[:, 768:896]
    logw = -DECAY_SCALE * jax.nn.sigmoid(w0_ref[d:d + 1, :] + _dot(jnp.tanh(wa), w2_ref[d]))
    ag = jax.nn.sigmoid(a0_ref[d:d + 1, :] + _dot(wa, a2_ref[d]))
    tri = jnp.where(upto_c, 1.0, 0.0).astype(BF16)
    cum = sum(jnp.dot(tri, piece, preferred_element_type=F32) for piece in _split3(logw))
    st[ch] = dict(zm=zm, logw=logw, ag=ag, cum=cum)

  def stage0_rest(ch):
    d, bi = ch
    e = st[ch]
    zm, logw, ag, cum = e["zm"], e["logw"], e["ag"], e["cum"]
    r = zm[:, 0:256]
    k = zm[:, 256:512]
    v = zm[:, 512:768]
    kkr = k * kkp
    kkn = kkr * jnp.minimum(lax.rsqrt(_head_sum(kkr * kkr)), 1e12)
    kt = k * (1.0 + (ag - 1.0) * kap)
    bonus = _head_sum(r * kt * rho_ref[d:d + 1, :]) * v
    tot = cum[c - 1:c, :] if d == 0 else cum[0:1, :]
    e_neg = jnp.exp(-cum)
    e_rest = jnp.exp(tot - cum)
    bvec = kkn * ag
    at = -kkn * jnp.exp(cum - logw)
    rt = r * jnp.exp(cum)
    st[ch] = dict(
        v=v, bonus=bonus, g_all=jnp.exp(tot), at=at, rt=rt,
        x=jnp.concatenate([stack(at), stack(rt)], axis=0),
        y=jnp.concatenate([stack(bvec * e_neg), stack(kt * e_neg)], axis=0),
        bk=jnp.concatenate([bvec * e_rest, kt * e_rest], axis=0),
        vm=stack(v))

  def stage1(ch):
    d, bi = ch
    e = st[ch]
    before = (ss < ts) if d == 0 else (ss > ts)
    upto = (ss <= ts) if d == 0 else (ss >= ts)
    a_all = _dot_nt(e["x"], e["y"])
    e["lab"] = jnp.where(before, a_all[0:sr, 0:sr], 0.0)
    lak = jnp.where(before, a_all[0:sr, sr:2 * sr], 0.0)
    e["p"] = jnp.concatenate([jnp.where(upto, a_all[sr:2 * sr, 0:sr], 0.0),
                              jnp.where(upto, a_all[sr:2 * sr, sr:2 * sr], 0.0)], axis=-1)
    s0 = s_sc[d, bi]
    e["s0"] = s0
    ar = _dot_nt(jnp.concatenate([e["at"], e["rt"]], axis=0), s0)
    e["rh"] = ar[c:2 * c]
    e["xs"] = stack(ar[0:c]) + _dot(lak, e["vm"])

  def stage2(group):
    for ch in group:
      e = st[ch]
      e["lp"] = _dot(e["lab"], e["lab"])
      e["t"] = eye + e["lab"]
    for it in range(4):
      for ch in group:
        e = st[ch]
        lp = e["lp"]
        e["t"] = e["t"] + _dot(e["t"], lp)
        if it < 3:
          e["lp"] = _dot(lp, lp)

  def stage3(group):
    for ch in group:
      e = st[ch]
      e["us"] = _dot(e["t"], e["xs"])
    for ch in group:
      d, bi = ch
      e = st[ch]
      ys = _dot(e["p"], jnp.concatenate([e["us"], e["vm"]], axis=0))
      y = e["rh"] + unstack(ys)
      u = unstack(e["us"])
      upd = _dot_tn(jnp.concatenate([u, e["v"]], axis=0), e["bk"])
      s_sc[d, bi] = e["s0"] * e["g_all"] + jnp.where(blockdiag, upd, 0.0)
      y_ref = yf_ref if d == 0 else yb_ref
      for s in range(2):
        sl = slice(s * LANES, (s + 1) * LANES)
        y_ref[bi, :, sl] = (_head_norm_slab(y[:, sl], lo) * lng[0:1, sl] + lnb[0:1, sl]
                            + e["bonus"][:, sl])

  chains = [(d, bi) for d in range(2) for bi in range(nbatch)]
  for ch in chains:
    stage0_mm(ch)
  for ch in chains:
    stage0_rest(ch)
  for ch in chains:
    stage1(ch)
  stage2(chains)
  stage3(chains)


def _rwkv(dd, mu, w0, w2p, a0, a2p, rho, kk, ka, lng, lnb, dims):
  b, n, ct = dims
  c = RWKV_CHUNK
  sa = n + ct
  bpb = sa // c
  ncx = ct // c
  zw = 768 + LANES
  f, r = _scan_maps(bpb, ncx)
  fwd = lambda j: (0, f(j), 0)
  bwd = lambda j: (0, r(j), 0)
  const2 = lambda j: (0, 0)
  const3 = lambda j: (0, 0, 0)
  d3 = dd.reshape(b, sa, D_COLS)

  yf, yb = pl.pallas_call(
      functools.partial(_rwkv_kernel, ncx=ncx, nbatch=b),
      grid=(bpb,),
      in_specs=[
          pl.BlockSpec((b, c, D_COLS), fwd),
          pl.BlockSpec((b, c, D_COLS), bwd),
          pl.BlockSpec((2, zw), const2),
          pl.BlockSpec((2, GROUP_W), const2),
          pl.BlockSpec((2, LANES, GROUP_W), const3),
          pl.BlockSpec((2, GROUP_W), const2),
          pl.BlockSpec((2, LANES, GROUP_W), const3),
          pl.BlockSpec((2, GROUP_W), const2),
          pl.BlockSpec((1, GROUP_W), const2),
          pl.BlockSpec((1, GROUP_W), const2),
          pl.BlockSpec((1, GROUP_W), const2),
          pl.BlockSpec((1, GROUP_W), const2),
      ],
      out_specs=[pl.BlockSpec((b, c, GROUP_W), fwd), pl.BlockSpec((b, c, GROUP_W), bwd)],
      out_shape=[jax.ShapeDtypeStruct((b, sa, GROUP_W), F32)] * 2,
      scratch_shapes=[pltpu.VMEM((2, b, GROUP_W, GROUP_W), F32), pltpu.VMEM((2, b, c + 16, zw), F32)],
      compiler_params=pltpu.CompilerParams(
          dimension_semantics=("arbitrary",), vmem_limit_bytes=VMEM_LIMIT),
      name="rwkv7",
  )(d3, d3, mu, w0, w2p, a0, a2p, rho, kk, ka, lng, lnb)
  return yf.reshape(b * sa, GROUP_W), yb.reshape(b * sa, GROUP_W)


_Q_HEAD_ORDER = (0, 2, 1, 3)


def _q_perm():
  return np.concatenate([np.arange(h * HEAD_DIM, (h + 1) * HEAD_DIM) for h in _Q_HEAD_ORDER])


def _proj_col_perm():
  qp = _q_perm()
  cols = [qp, np.arange(256, 512), 512 + qp, np.arange(768, 1024), np.arange(1024, 2304)]
  d0 = 2304
  cols.append(np.arange(d0, d0 + 768 + 128))
  wdf, wdb, adf, adb = (np.arange(d0 + 896 + 64 * t, d0 + 896 + 64 * (t + 1)) for t in range(4))
  cols += [wdf, adf, wdb, adb]
  return np.concatenate(cols)


def _rope_tables(n, tail):
  def cos_sin(pos, dim):
    inv = 1.0 / (ROPE_BASE ** (np.arange(0, dim, 2, dtype=np.float64) / dim))
    ang = pos.astype(np.float64)[:, None] * inv[None, :]
    return np.cos(ang), np.sin(ang)

  rows = n // GRID_W
  row = np.repeat(np.arange(rows), GRID_W)
  col = np.arange(rows * GRID_W) % GRID_W
  cr, sr = cos_sin(row, HEAD_DIM // 2)
  cc, sc = cos_sin(col, HEAD_DIM // 2)
  cq, sq = cos_sin(np.arange(n), HEAD_DIM)
  cax = np.concatenate([cr, cr, cc, cc] * 2, axis=-1)
  sax = np.concatenate([-sr, sr, -sc, sc] * 2, axis=-1)
  csq = np.concatenate([cq, cq] * 2, axis=-1)
  ssq = np.concatenate([-sq, sq] * 2, axis=-1)
  ones = np.ones((tail, LANES))
  zeros = np.zeros((tail, LANES))
  return tuple(np.concatenate(t).astype(np.float32)
               for t in ([cax, ones], [sax, zeros], [csq, ones], [ssq, zeros]))


def _retention_tables():
  c = RET_CHUNK
  lg = np.log1p(-np.exp2(-5.0 - np.arange(4, dtype=np.float64)))
  idx = np.arange(c, dtype=np.float64)
  rel = idx[:, None] - idx[None, :]
  dfw = np.where(rel[None] >= 0, np.exp(np.maximum(rel, 0.0)[None] * lg[:, None, None]), 0.0)
  dec = np.stack([dfw, np.swapaxes(dfw, 1, 2)])
  lane_lg = np.repeat(lg, HEAD_DIM)[None, :]
  qw_f = np.exp((idx + 1.0)[:, None] * lane_lg)
  kw_f = np.exp((c - 1.0 - idx)[:, None] * lane_lg)
  qw_b = np.exp((c - idx)[:, None] * lane_lg)
  kw_b = np.exp(idx[:, None] * lane_lg)
  cd = np.broadcast_to(np.exp(c * lane_lg), (c, GROUP_W))
  lw = np.stack([np.stack([qw_f, kw_f, cd]), np.stack([qw_b, kw_b, cd])])
  return dec.astype(np.float32), lw.astype(np.float32)


def kernel(x, c, ctx, c_ctx, w_mod, b_mod, norm_g, ffn_w_in, ffn_w_out, w_in, w_out, attn_sink,
           qk_norm_g, ret_norm_g, rwkv_mu, rwkv_w0, rwkv_w2, rwkv_a0, rwkv_a2, rwkv_rho, rwkv_k_k,
           rwkv_k_a, rwkv_g2, rwkv_ln_g, rwkv_ln_b, final_norm_g):
  b, n, d = x.shape
  ct = ctx.shape[1]
  depth = w_mod.shape[0]
  assert d == D_MODEL and b < MOD_ROWS
  assert n % ROW_TILE == 0 and ct % ROW_TILE == 0 and n % KEY_CHUNK == 0 and ct % Q_TILE == 0
  assert n % RET_CHUNK == 0 and ct % RET_CHUNK == 0 and n % GRID_W == 0 and KEY_CHUNK % ct == 0
  dims = (b, n, ct)

  cs = jnp.zeros((MOD_ROWS, d), F32).at[:b].set(c).at[b].set(c_ctx)
  mod = _modulation(cs, w_mod, b_mod).reshape(depth, MOD_ROWS, N_MOD, d)

  tabs = _rope_tables(n, ROW_TILE)
  dec, lw = _retention_tables()
  col_perm = _proj_col_perm()
  q_perm = _q_perm()
  out_perm = np.concatenate([q_perm, 256 + q_perm, np.arange(512, 1024)])
  zpad = jnp.zeros((2, A_RANK, GROUP_W), F32)

  xs = (x.reshape(b * n, d), ctx.reshape(b * ct, d))
  for l in range(depth):
    last = l == depth - 1
    w1a, w2a = ffn_w_in[l, 0].astype(BF16), ffn_w_out[l, 0].astype(BF16)
    w1b, w2b = ffn_w_in[l, 1].astype(BF16), ffn_w_out[l, 1].astype(BF16)
    win = w_in[l][:, col_perm].astype(BF16)
    wo = w_out[l][out_perm, :].astype(BF16)
    qkg = jnp.tile(qk_norm_g[l], (1, GROUP_W // HEAD_DIM))
    sink_tab = jnp.broadcast_to((attn_sink[l] * LOG2E)[jnp.asarray(_Q_HEAD_ORDER)][:, None], (4, LANES))
    w2p = jnp.concatenate([rwkv_w2[l], zpad], axis=1)
    a2p = jnp.concatenate([zpad, rwkv_a2[l]], axis=1)

    xs, qa, kva, qb, kvb, rc, gc, dd = _ffn_proj(
        xs, mod[l], norm_g[l], w1a, w2a, win, qkg, tabs, dims)
    oa = _win_attn(qa, kva, sink_tab, dims)
    ob = _glob_attn(qb, kvb, _score_bound(qk_norm_g[l]), dims)
    cf, cb = _retention(rc, gc, dec, lw, ret_norm_g[l][None, :], dims)
    yf, yb = _rwkv(dd, rwkv_mu[l], rwkv_w0[l], w2p, rwkv_a0[l], a2p,
                   rwkv_rho[l].reshape(2, GROUP_W), rwkv_k_k[l][None, :], rwkv_k_a[l][None, :],
                   rwkv_ln_g[l][None, :], rwkv_ln_b[l][None, :], dims)
    xs = _out_ffn(xs, mod[l], norm_g[l], oa, ob, cf, cb, yf, yb, dd, rwkv_g2[l].astype(BF16), wo,
                  w1b, w2b, final_norm_g[None, :], dims, final=last)
  return xs.reshape(b, n, d)
```

```python
import functools

import numpy as np
import jax
import jax.numpy as jnp
from jax import lax
from jax.experimental import pallas as pl
from jax.experimental.pallas import tpu as pltpu

F32 = jnp.float32
BF16 = jnp.bfloat16

D_MODEL = 1024
HEAD_DIM = 64
N_HEADS = 4
GROUP_W = 256
KV_W = 128
GRID_W = 64
BLOCK = 128
D_FF = 2816
A_RANK = 64
G_RANK = 128
N_MOD = 9
ROPE_BASE = 10000.0
RMS_EPS = 1e-6
GN_EPS = 64e-5
DECAY_SCALE = 0.6065306597126334
PROJ_COLS = 3456
D_COLS = 1152
LANES = 128
MOD_ROWS = 8
BF16_ROWS = 16

ROW_TILE = 256
FF_CHUNK = 2816
Q_TILE = 256
WIN_Q = 2 * BLOCK
KEY_CHUNK = 2048
RET_CHUNK = 256
RWKV_CHUNK = 32
VMEM_LIMIT = 56 * 1024 * 1024

NEG_BIG = -1e30
LOG2E = 1.4426950408889634
BOUND_MARGIN = 1.02
UNDERFLOW_GUARD = 2.0 ** -100
HIGHEST = lax.Precision.HIGHEST


def _dot(a, b):
  return jnp.dot(a.astype(BF16), b.astype(BF16), preferred_element_type=F32)


def _dot_nt(a, b):
  return lax.dot_general(a.astype(BF16), b.astype(BF16), (((1,), (1,)), ((), ())),
                         preferred_element_type=F32)


def _dot_tn(a, b):
  return jnp.dot(a.astype(F32).T.astype(BF16), b.astype(BF16), preferred_element_type=F32)


def _silu(x):
  return x * jax.nn.sigmoid(x)


def _lane_lo(shape):
  return (lax.broadcasted_iota(jnp.int32, shape, len(shape) - 1) % LANES) < HEAD_DIM


def _half_sum(x, lo):
  s_lo = jnp.sum(jnp.where(lo, x, 0.0), axis=-1, keepdims=True)
  s_all = jnp.sum(x, axis=-1, keepdims=True)
  return jnp.where(lo, s_lo, s_all - s_lo)


def _head_sum(x):
  lo = _lane_lo((1, LANES))
  return jnp.concatenate([_half_sum(x[:, o:o + LANES], lo) for o in range(0, GROUP_W, LANES)], axis=-1)


def _head_norm_slab(y, lo):
  mu = _half_sum(y, lo) * (1.0 / HEAD_DIM)
  d = y - mu
  var = _half_sum(d * d, lo) * (1.0 / HEAD_DIM)
  return d * lax.rsqrt(var + GN_EPS)


def _rope_slab(x, c, s, half):
  first = (lax.broadcasted_iota(jnp.int32, x.shape, 1) % (2 * half)) < half
  partner = jnp.where(first, pltpu.roll(x, LANES - half, axis=1), pltpu.roll(x, half, axis=1))
  return x * c + partner * s


def _mod_kernel(c_ref, w_ref, b_ref, o_ref):
  s = _silu(c_ref[...])
  o_ref[...] = jnp.dot(s, w_ref[...], precision=HIGHEST, preferred_element_type=F32) + b_ref[...]


def _modulation(cs, w_mod, b_mod):
  depth = w_mod.shape[0]
  return pl.pallas_call(
      _mod_kernel,
      grid=(depth, N_MOD),
      in_specs=[
          pl.BlockSpec((MOD_ROWS, D_MODEL), lambda l, j: (0, 0)),
          pl.BlockSpec((None, D_MODEL, D_MODEL), lambda l, j: (l, 0, j)),
          pl.BlockSpec((None, 1, D_MODEL), lambda l, j: (l, 0, j)),
      ],
      out_specs=pl.BlockSpec((None, MOD_ROWS, D_MODEL), lambda l, j: (l, 0, j)),
      out_shape=jax.ShapeDtypeStruct((depth, MOD_ROWS, N_MOD * D_MODEL), F32),
      compiler_params=pltpu.CompilerParams(
          dimension_semantics=("arbitrary", "arbitrary"), vmem_limit_bytes=VMEM_LIMIT),
      name="modulation",
  )(cs, w_mod, b_mod.reshape(depth, 1, N_MOD * D_MODEL))


def _rms_mod(x, g, shift, scale):
  h = x * lax.rsqrt(jnp.mean(x * x, axis=-1, keepdims=True) + RMS_EPS) * g
  return h * (1.0 + scale) + shift


def _swiglu(h, w1_ref, w2_ref):
  hb = h.astype(BF16)
  acc = None
  for c0 in range(0, D_FF, FF_CHUNK):
    u1 = jnp.dot(hb, w1_ref[:, c0:c0 + FF_CHUNK], preferred_element_type=F32)
    u2 = jnp.dot(hb, w1_ref[:, D_FF + c0:D_FF + c0 + FF_CHUNK], preferred_element_type=F32)
    a = (_silu(u1) * u2).astype(BF16)
    part = jnp.dot(a, w2_ref[c0:c0 + FF_CHUNK, :], preferred_element_type=F32)
    acc = part if acc is None else acc + part
  return acc


def _head_rms(x, g):
  lo = _lane_lo((1, LANES))
  sq = x * x
  ms = jnp.concatenate([_half_sum(sq[:, o:o + LANES], lo) for o in range(0, x.shape[-1], LANES)],
                       axis=-1) * (1.0 / HEAD_DIM)
  return x * lax.rsqrt(ms + RMS_EPS) * g


def _split3(x):
  hi = x.astype(BF16)
  r1 = x - hi.astype(F32)
  mid = r1.astype(BF16)
  lo = (r1 - mid.astype(F32)).astype(BF16)
  return hi, mid, lo


def _rope_wide(x, c, s, half):
  parts = [_rope_slab(x[:, o:o + LANES], c, s, half) for o in range(0, x.shape[-1], LANES)]
  return parts[0] if len(parts) == 1 else jnp.concatenate(parts, axis=-1)


def _ffn_proj_kernel(*refs, tpb, ctt, split):
  if split:
    x_ref, xc_ref, *refs = refs
    x = jnp.where(pl.program_id(0) % tpb < ctt, xc_ref[...], x_ref[...])
  else:
    x_ref, *refs = refs
    x = x_ref[...]
  (mod_ref, ng_ref, w1_ref, w2_ref, win_ref, qkg_ref, cax_ref, sax_ref, csq_ref, ssq_ref,
   xo_ref, qa_ref, kva_ref, qb_ref, kvb_ref, rc_ref, gc_ref, dd_ref) = refs
  m = mod_ref[...]
  ng = ng_ref[...]
  h = _rms_mod(x, ng[0:1], m[0:1], m[1:2])
  x1 = x + 0.5 * m[2:3] * _swiglu(h, w1_ref, w2_ref)
  xo_ref[...] = x1
  hb = _rms_mod(x1, ng[1:2], m[3:4], m[4:5]).astype(BF16)

  cax, sax = cax_ref[...], sax_ref[...]
  csq, ssq = csq_ref[...], ssq_ref[...]
  qkg = qkg_ref[...]
  scale = HEAD_DIM ** -0.5

  p = jnp.dot(hb, win_ref[:, 0:512], preferred_element_type=F32)
  qa_ref[...] = (_rope_wide(p[:, 0:256], cax, sax, 16) * (scale * LOG2E)).astype(BF16)
  kva_ref[:, 0:128] = _rope_wide(p[:, 256:384], cax, sax, 16).astype(BF16)
  kva_ref[:, 128:256] = p[:, 384:512].astype(BF16)
  p = jnp.dot(hb, win_ref[:, 512:1024], preferred_element_type=F32)
  qn = _head_rms(p[:, 0:256], qkg[0:1, :])
  kn = _head_rms(p[:, 256:384], qkg[1:2, 0:128])
  qb_ref[...] = (_rope_wide(qn, cax, sax, 16) * (scale * LOG2E)).astype(BF16)
  kvb_ref[:, 0:128] = _rope_wide(kn, cax, sax, 16).astype(BF16)
  kvb_ref[:, 128:256] = p[:, 384:512].astype(BF16)
  p = jnp.dot(hb, win_ref[:, 1024:2304], preferred_element_type=F32)
  rc_ref[:, 0:256] = _rope_wide(p[:, 0:256], csq, ssq, 32).astype(BF16)
  rc_ref[:, 256:512] = (_rope_wide(p[:, 256:512], csq, ssq, 32) * scale).astype(BF16)
  rc_ref[:, 512:768] = p[:, 512:768].astype(BF16)
  gc_ref[...] = p[:, 768:1280]
  dd_ref[...] = jnp.dot(hb, win_ref[:, 2304:3456], preferred_element_type=F32)


def _ffn_proj(xs, mod, ng, w1, w2, win, qkg, tabs, dims):
  b, n, ct = dims
  rows = b * (n + ct)
  tm = ROW_TILE
  tpb = (n + ct) // tm
  ctt = ct // tm
  ident = n // tm

  def row(i):
    return (i, 0)

  def const(i):
    return (0, 0)

  def mod_idx(i):
    return (jnp.where(i % tpb < ctt, b, i // tpb), 0, 0)

  def tab_idx(i):
    t = i % tpb
    return (jnp.where(t < ctt, ident, t - ctt), 0)

  split = isinstance(xs, tuple)
  if split:
    def lat_row(i):
      return ((i // tpb) * (n // tm) + jnp.maximum(i % tpb - ctt, 0), 0)

    def ctx_row(i):
      return ((i // tpb) * ctt + jnp.minimum(i % tpb, ctt - 1), 0)

    x_specs = [pl.BlockSpec((tm, D_MODEL), lat_row), pl.BlockSpec((tm, D_MODEL), ctx_row)]
    xs = list(xs)
  else:
    x_specs = [pl.BlockSpec((tm, D_MODEL), row)]
    xs = [xs]

  widths = (D_MODEL, 256, 256, 256, 256, 768, 512, D_COLS)
  dtypes = (F32, BF16, BF16, BF16, BF16, BF16, F32, F32)
  return pl.pallas_call(
      functools.partial(_ffn_proj_kernel, tpb=tpb, ctt=ctt, split=split),
      grid=(rows // tm,),
      in_specs=x_specs + [
          pl.BlockSpec((None, N_MOD, D_MODEL), mod_idx),
          pl.BlockSpec((3, D_MODEL), const),
          pl.BlockSpec((D_MODEL, 2 * D_FF), const),
          pl.BlockSpec((D_FF, D_MODEL), const),
          pl.BlockSpec((D_MODEL, PROJ_COLS), const),
          pl.BlockSpec((2, GROUP_W), const),
          pl.BlockSpec((tm, LANES), tab_idx),
          pl.BlockSpec((tm, LANES), tab_idx),
          pl.BlockSpec((tm, LANES), tab_idx),
          pl.BlockSpec((tm, LANES), tab_idx),
      ],
      out_specs=[pl.BlockSpec((tm, w), row) for w in widths],
      out_shape=[jax.ShapeDtypeStruct((rows, w), dt) for w, dt in zip(widths, dtypes)],
      compiler_params=pltpu.CompilerParams(
          dimension_semantics=("arbitrary",), vmem_limit_bytes=VMEM_LIMIT),
      name="ffn_proj",
  )(*xs, mod, ng, w1, w2, win, qkg, *tabs)


def _out_ffn_kernel(x_ref, mod_ref, ng_ref, oa_ref, ob_ref, cf_ref, cb_ref, yf_ref, yb_ref,
                    gd_ref, g2_ref, wo_ref, w1_ref, w2_ref, fg_ref, xo_ref, *, final):
  x = x_ref[...]
  m = mod_ref[...]
  ng = ng_ref[...]
  gate = _dot(jax.nn.sigmoid(gd_ref[...]), g2_ref[...])
  od = (yf_ref[...] + yb_ref[...]) * gate
  oc = cf_ref[...] + cb_ref[...]
  y = jnp.dot(oa_ref[...], wo_ref[0:256, :], preferred_element_type=F32)
  y = y + jnp.dot(ob_ref[...], wo_ref[256:512, :], preferred_element_type=F32)
  y = y + jnp.dot(oc.astype(BF16), wo_ref[512:768, :], preferred_element_type=F32)
  y = y + jnp.dot(od.astype(BF16), wo_ref[768:1024, :], preferred_element_type=F32)
  x1 = x + m[5:6] * y
  h = _rms_mod(x1, ng[2:3], m[6:7], m[7:8])
  x2 = x1 + 0.5 * m[8:9] * _swiglu(h, w1_ref, w2_ref)
  if final:
    x2 = x2 * lax.rsqrt(jnp.mean(x2 * x2, axis=-1, keepdims=True) + RMS_EPS) * fg_ref[...]
  xo_ref[...] = x2


def _out_ffn(x, mod, ng, oa, ob, cf, cb, yf, yb, dd, g2, wo, w1, w2, fg, dims, final):
  b, n, ct = dims
  tm = ROW_TILE
  tpb = (n + ct) // tm
  ctt = ct // tm
  if final:
    grid = (b, n // tm)
    row = lambda bi, j: (bi * tpb + ctt + j, 0)
    out_row = lambda bi, j: (bi * (n // tm) + j, 0)
    mod_idx = lambda bi, j: (bi, 0, 0)
    gd_idx = lambda bi, j: (bi * tpb + ctt + j, 768 // G_RANK)
    out_rows = b * n
  else:
    grid = (b, tpb)
    row = lambda bi, j: (bi * tpb + j, 0)
    out_row = row
    mod_idx = lambda bi, j: (jnp.where(j < ctt, b, bi), 0, 0)
    gd_idx = lambda bi, j: (bi * tpb + j, 768 // G_RANK)
    out_rows = x.shape[0]
  const = lambda bi, j: (0, 0)

  return pl.pallas_call(
      functools.partial(_out_ffn_kernel, final=final),
      grid=grid,
      in_specs=[
          pl.BlockSpec((tm, D_MODEL), row),
          pl.BlockSpec((None, N_MOD, D_MODEL), mod_idx),
          pl.BlockSpec((3, D_MODEL), const),
          pl.BlockSpec((tm, GROUP_W), row),
          pl.BlockSpec((tm, GROUP_W), row),
          pl.BlockSpec((tm, GROUP_W), row),
          pl.BlockSpec((tm, GROUP_W), row),
          pl.BlockSpec((tm, GROUP_W), row),
          pl.BlockSpec((tm, GROUP_W), row),
          pl.BlockSpec((tm, G_RANK), gd_idx),
          pl.BlockSpec((G_RANK, GROUP_W), const),
          pl.BlockSpec((D_MODEL, D_MODEL), const),
          pl.BlockSpec((D_MODEL, 2 * D_FF), const),
          pl.BlockSpec((D_FF, D_MODEL), const),
          pl.BlockSpec((1, D_MODEL), const),
      ],
      out_specs=pl.BlockSpec((tm, D_MODEL), out_row),
      out_shape=jax.ShapeDtypeStruct((out_rows, D_MODEL), F32),
      compiler_params=pltpu.CompilerParams(
          dimension_semantics=("arbitrary", "arbitrary"), vmem_limit_bytes=VMEM_LIMIT),
      name="out_ffn_final" if final else "out_ffn",
  )(x, mod, ng, oa, ob, cf, cb, yf, yb, dd, g2, wo, w1, w2, fg)


def _win_attn_kernel(q_ref, k0_ref, k1_ref, k2_ref, k3_ref, kx_ref, sink_ref, o_ref, *, spb, cs, nb):
  t = pl.program_id(0) % spb
  is_lat = t >= cs
  n = 2 * (t - cs)
  big = 4 * BLOCK
  off_lat = jnp.where(is_lat, 0, 2 * big)
  off0 = jnp.where(jnp.logical_and(is_lat, n > 0), 0, 2 * big)
  off3 = jnp.where(jnp.logical_and(is_lat, n + 2 < nb), 0, 2 * big)
  r = lax.broadcasted_iota(jnp.int32, (BLOCK, WIN_Q), 0)
  lane = lax.broadcasted_iota(jnp.int32, (BLOCK, WIN_Q), 1)
  d = r - lane % BLOCK
  first = lane < BLOCK
  segs = [
      (k0_ref, d >= jnp.where(first, 0, big) + off0),
      (k1_ref, d >= jnp.where(first, -big, 0) + off_lat),
      (k2_ref, -d >= jnp.where(first, 0, -big) + off_lat),
      (k3_ref, -d >= jnp.where(first, big, 0) + off3),
      (kx_ref, None),
  ]
  lo = _lane_lo((1, LANES))
  qms = []
  for s in range(2):
    qs = q_ref[:, s * LANES:(s + 1) * LANES]
    for hf in range(2):
      qms.append(jnp.where(lo if hf == 0 else jnp.logical_not(lo), qs, jnp.zeros_like(qs)))
  all_scores = []
  for qm in qms:
    scs = []
    for ref, valid in segs:
      sc = _dot_nt(ref[:, 0:LANES], qm)
      scs.append(sc if valid is None else jnp.where(valid, sc, NEG_BIG))
    all_scores.append(scs)
  vts = [ref[:, LANES:2 * LANES].astype(F32).T.astype(BF16) for ref, _ in segs]
  outs = []
  for h, scs in enumerate(all_scores):
    hf = h % 2
    snk = sink_ref[h:h + 1, 0:1]
    mx = jnp.maximum(scs[0].max(axis=0, keepdims=True), snk)
    for sc in scs[1:]:
      mx = jnp.maximum(mx, sc.max(axis=0, keepdims=True))
    den = jnp.exp2(snk - mx)
    acc = None
    for sc, vt in zip(scs, vts):
      p = jnp.exp2(sc - mx)
      den = den + p.sum(axis=0, keepdims=True)
      pv = jnp.dot(vt[hf * HEAD_DIM:(hf + 1) * HEAD_DIM, :], p.astype(BF16),
                   preferred_element_type=F32)
      acc = pv if acc is None else acc + pv
    outs.append(acc / den)
  for s in range(2):
    ot = jnp.concatenate([outs[2 * s], outs[2 * s + 1]], axis=0)
    o_ref[:, s * LANES:(s + 1) * LANES] = ot.T.astype(BF16)


def _win_attn(qa, kva, sink_tab, dims):
  b, n, ct = dims
  rows = qa.shape[0]
  nb = n // BLOCK
  cb = ct // BLOCK
  bpb = nb + cb
  spb = (n + ct) // WIN_Q
  cs = ct // WIN_Q

  def key_block(j):
    def idx(i):
      blk = cb + 2 * (i % spb - cs) + j
      return ((i // spb) * bpb + jnp.clip(blk, cb, bpb - 1), 0)
    return idx

  return pl.pallas_call(
      functools.partial(_win_attn_kernel, spb=spb, cs=cs, nb=nb),
      grid=(rows // WIN_Q,),
      in_specs=[pl.BlockSpec((WIN_Q, GROUP_W), lambda i: (i, 0))]
      + [pl.BlockSpec((BLOCK, GROUP_W), key_block(j)) for j in (-1, 0, 1, 2)]
      + [pl.BlockSpec((ct, GROUP_W), lambda i: ((i // spb) * ((n + ct) // ct), 0)),
         pl.BlockSpec((4, LANES), lambda i: (0, 0))],
      out_specs=pl.BlockSpec((WIN_Q, GROUP_W), lambda i: (i, 0)),
      out_shape=jax.ShapeDtypeStruct((rows, GROUP_W), BF16),
      compiler_params=pltpu.CompilerParams(
          dimension_semantics=("arbitrary",), vmem_limit_bytes=VMEM_LIMIT),
      name="window_attn",
  )(qa, kva, kva, kva, kva, kva, sink_tab)


def _glob_attn_kernel(q_ref, kv_ref, bnd_ref, o_ref, m_sc, l_sc, acc_sc, *, ct, ctq, n_chunks):
  lo = _lane_lo((1, LANES))
  qms = []
  for s in range(2):
    qs = q_ref[:, s * LANES:(s + 1) * LANES]
    for hf in range(2):
      qms.append(jnp.where(lo if hf == 0 else jnp.logical_not(lo), qs, jnp.zeros_like(qs)))
  bound = bnd_ref[0:1, 0:1]
  is_ctx = pl.program_id(1) < ctq
  lat_chunks = [(ct + j * KEY_CHUNK, KEY_CHUNK) for j in range(n_chunks)] + [(0, ct)]

  def scores(start, nk):
    k = kv_ref[start:start + nk, 0:LANES]
    return [_dot_nt(k, qm) for qm in qms]

  def v_t(start, nk):
    return kv_ref[pl.ds(start, nk), LANES:2 * LANES].astype(F32).T.astype(BF16)

  def consume(scs, start, nk):
    vt = v_t(start, nk)
    ones = jnp.ones((BF16_ROWS, nk), BF16)
    lhs = [jnp.concatenate([vt[hf * HEAD_DIM:(hf + 1) * HEAD_DIM, :], ones], axis=0) for hf in range(2)]
    for h, sc in enumerate(scs):
      p = jnp.exp2(sc - bound).astype(BF16)
      pv = jnp.dot(lhs[h % 2], p, preferred_element_type=F32)
      l_sc[h] = l_sc[h] + pv[HEAD_DIM:HEAD_DIM + 1]
      acc_sc[h] = acc_sc[h] + pv[0:HEAD_DIM]

  def run(chunks):
    scs = scores(*chunks[0])
    for i, ch in enumerate(chunks):
      nxt = scores(*chunks[i + 1]) if i + 1 < len(chunks) else None
      consume(scs, *ch)
      scs = nxt

  l_sc[...] = jnp.zeros(l_sc.shape, F32)
  acc_sc[...] = jnp.zeros(acc_sc.shape, F32)

  @pl.when(is_ctx)
  def _():
    run([(0, ct)])

  @pl.when(jnp.logical_not(is_ctx))
  def _():
    run(lat_chunks)

  @pl.when(jnp.min(l_sc[...]) < UNDERFLOW_GUARD)
  def _():
    m_sc[...] = jnp.full(m_sc.shape, NEG_BIG, F32)
    l_sc[...] = jnp.zeros(l_sc.shape, F32)
    acc_sc[...] = jnp.zeros(acc_sc.shape, F32)

    def online(start, nk):
      k = kv_ref[pl.ds(start, nk), 0:LANES]
      vt = v_t(start, nk)
      for h, qm in enumerate(qms):
        hf = h % 2
        sc = _dot_nt(k, qm)
        m_prev = m_sc[h]
        m_new = jnp.maximum(m_prev, sc.max(axis=0, keepdims=True))
        alpha = jnp.exp2(m_prev - m_new)
        p = jnp.exp2(sc - m_new)
        l_sc[h] = alpha * l_sc[h] + p.sum(axis=0, keepdims=True)
        pv = jnp.dot(vt[hf * HEAD_DIM:(hf + 1) * HEAD_DIM, :], p.astype(BF16),
                     preferred_element_type=F32)
        acc_sc[h] = alpha * acc_sc[h] + pv
        m_sc[h] = m_new

    def body(j, carry):
      online(pl.multiple_of(ct + j * KEY_CHUNK, ct), KEY_CHUNK)
      return carry

    lax.fori_loop(0, jnp.where(is_ctx, 0, n_chunks), body, 0)
    online(0, ct)

  for s in range(2):
    ot = jnp.concatenate([acc_sc[2 * s] / l_sc[2 * s], acc_sc[2 * s + 1] / l_sc[2 * s + 1]], axis=0)
    o_ref[:, s * LANES:(s + 1) * LANES] = ot.T.astype(BF16)


def _score_bound(qk_g):
  gq = jnp.max(jnp.abs(qk_g[0]))
  gk = jnp.max(jnp.abs(qk_g[1]))
  bound = BOUND_MARGIN * HEAD_DIM * gq * gk * (HEAD_DIM ** -0.5 * LOG2E)
  return jnp.broadcast_to(bound, (MOD_ROWS, LANES)).astype(F32)


def _glob_attn(qb, kvb, bound, dims):
  b, n, ct = dims
  tq = Q_TILE
  sa = n + ct
  return pl.pallas_call(
      functools.partial(_glob_attn_kernel, ct=ct, ctq=ct // tq, n_chunks=n // KEY_CHUNK),
      grid=(b, sa // tq),
      in_specs=[
          pl.BlockSpec((tq, GROUP_W), lambda bi, j: (bi * (sa // tq) + j, 0)),
          pl.BlockSpec((sa, GROUP_W), lambda bi, j: (bi, 0)),
          pl.BlockSpec((MOD_ROWS, LANES), lambda bi, j: (0, 0)),
      ],
      out_specs=pl.BlockSpec((tq, GROUP_W), lambda bi, j: (bi * (sa // tq) + j, 0)),
      out_shape=jax.ShapeDtypeStruct((b * sa, GROUP_W), BF16),
      scratch_shapes=[pltpu.VMEM((4, 1, tq), F32), pltpu.VMEM((4, 1, tq), F32),
                      pltpu.VMEM((4, HEAD_DIM, tq), F32)],
      compiler_params=pltpu.CompilerParams(
          dimension_semantics=("arbitrary", "arbitrary"), vmem_limit_bytes=VMEM_LIMIT),
      name="global_attn",
  )(qb, kvb, bound)


def _ret_kernel(rf_ref, rb_ref, gf_ref, gb_ref, dec_ref, lw_ref, ng_ref, of_ref, ob_ref,
                sf_sc, sb_sc):
  j = pl.program_id(1)

  @pl.when(j == 0)
  def _():
    sf_sc[...] = jnp.zeros(sf_sc.shape, F32)
    sb_sc[...] = jnp.zeros(sb_sc.shape, F32)

  lo = _lane_lo((1, LANES))
  rr = lax.broadcasted_iota(jnp.int32, (LANES, LANES), 0) // HEAD_DIM
  cc = lax.broadcasted_iota(jnp.int32, (LANES, LANES), 1) // HEAD_DIM
  blockdiag = rr == cc
  ng = ng_ref[...]

  for d, (r_ref, g_ref, o_ref, s_sc) in enumerate(
      ((rf_ref, gf_ref, of_ref, sf_sc), (rb_ref, gb_ref, ob_ref, sb_sc))):
    for s in range(2):
      sl = slice(s * LANES, (s + 1) * LANES)
      q = r_ref[:, s * LANES:(s + 1) * LANES]
      k = r_ref[:, 256 + s * LANES:256 + (s + 1) * LANES]
      v = r_ref[:, 512 + s * LANES:512 + (s + 1) * LANES]
      qw = lw_ref[d, 0][:, sl]
      kw = lw_ref[d, 1][:, sl]
      cd = lw_ref[d, 2][0:1, sl]
      st = s_sc[s]
      o = _dot(q.astype(F32) * qw, st)
      halves = []
      for hf in range(2):
        qm = jnp.where(lo if hf == 0 else jnp.logical_not(lo), q, jnp.zeros_like(q))
        att = _dot_nt(qm, k) * dec_ref[d, 2 * s + hf]
        halves.append(_dot(att, v))
      o = o + jnp.where(lo, halves[0], halves[1])
      u = _dot_tn(k.astype(F32) * kw, v)
      s_sc[s] = st * cd + jnp.where(blockdiag, u, 0.0)
      gate = g_ref[:, d * GROUP_W + s * LANES:d * GROUP_W + (s + 1) * LANES]
      o_ref[:, sl] = _head_norm_slab(o, lo) * ng[0:1, sl] * _silu(gate)


def _scan_maps(bpb, ncx):
  def fwd(j):
    return j

  def bwd(j):
    return jnp.where(j < ncx, ncx - 1 - j, bpb - 1 - (j - ncx))

  return fwd, bwd


def _retention(rc, gc, dec, lw, ng, dims):
  b, n, ct = dims
  c = RET_CHUNK
  rows = rc.shape[0]
  bpb = (n + ct) // c
  f, r = _scan_maps(bpb, ct // c)
  fwd = lambda bi, j: (bi * bpb + f(j), 0)
  bwd = lambda bi, j: (bi * bpb + r(j), 0)

  return pl.pallas_call(
      _ret_kernel,
      grid=(b, bpb),
      in_specs=[
          pl.BlockSpec((c, 768), fwd),
          pl.BlockSpec((c, 768), bwd),
          pl.BlockSpec((c, 512), fwd),
          pl.BlockSpec((c, 512), bwd),
          pl.BlockSpec((2, 4, c, c), lambda bi, j: (0, 0, 0, 0)),
          pl.BlockSpec((2, 3, c, GROUP_W), lambda bi, j: (0, 0, 0, 0)),
          pl.BlockSpec((1, GROUP_W), lambda bi, j: (0, 0)),
      ],
      out_specs=[pl.BlockSpec((c, GROUP_W), fwd), pl.BlockSpec((c, GROUP_W), bwd)],
      out_shape=[jax.ShapeDtypeStruct((rows, GROUP_W), F32)] * 2,
      scratch_shapes=[pltpu.VMEM((2, LANES, LANES), F32)] * 2,
      compiler_params=pltpu.CompilerParams(
          dimension_semantics=("arbitrary", "arbitrary"), vmem_limit_bytes=VMEM_LIMIT),
      name="retention",
  )(rc, rc, gc, gc, dec, lw, ng)


def _rwkv_kernel(df_ref, db_ref, mu_ref, w0_ref, w2_ref, a0_ref, a2_ref, rho_ref, kk_ref, ka_ref,
                 lng_ref, lnb_ref, yf_ref, yb_ref, s_sc, z_sc, *, ncx, nbatch):
  j = pl.program_id(0)
  c = RWKV_CHUNK
  sr = N_HEADS * c
  at_start = jnp.logical_or(j == 0, j == ncx)

  @pl.when(j == 0)
  def _():
    s_sc[...] = jnp.zeros(s_sc.shape, F32)

  @pl.when(at_start)
  def _():
    z_sc[:, :, 0:8, :] = jnp.zeros((2, nbatch, 8, z_sc.shape[3]), F32)
    z_sc[:, :, c + 8:c + 16, :] = jnp.zeros((2, nbatch, 8, z_sc.shape[3]), F32)

  lane_head = lax.broadcasted_iota(jnp.int32, (1, GROUP_W), 1) // HEAD_DIM
  head_masks = [lane_head == h for h in range(N_HEADS)]
  rr = lax.broadcasted_iota(jnp.int32, (GROUP_W, GROUP_W), 0) // HEAD_DIM
  cc = lax.broadcasted_iota(jnp.int32, (GROUP_W, GROUP_W), 1) // HEAD_DIM
  blockdiag = rr == cc
  ti = lax.broadcasted_iota(jnp.int32, (c, c), 0)
  si = lax.broadcasted_iota(jnp.int32, (c, c), 1)
  ts = lax.broadcasted_iota(jnp.int32, (sr, sr), 0)
  ss = lax.broadcasted_iota(jnp.int32, (sr, sr), 1)
  eye = jnp.where(ts == ss, 1.0, 0.0)
  ts, ss = ts % c, ss % c
  kkp = kk_ref[...]
  kap = ka_ref[...]
  lng = lng_ref[...]
  lnb = lnb_ref[...]
  lo = _lane_lo((1, LANES))

  def stack(x):
    return jnp.concatenate([jnp.where(m, x, 0.0) for m in head_masks], axis=0)

  def unstack(x):
    return x[0:c] + x[c:2 * c] + x[2 * c:3 * c] + x[3 * c:4 * c]

  st = {}

  def stage0_mm(ch):
    d, bi = ch
    d_ref = df_ref if d == 0 else db_ref
    upto_c = (si <= ti) if d == 0 else (si >= ti)
    z = jnp.concatenate([d_ref[bi, :, 0:768], d_ref[bi, :, 896 + d * LANES:1024 + d * LANES]], axis=-1)
    z_sc[d, bi, 8:c + 8, :] = z
    zs = z_sc[d, bi, 7:c + 7, :] if d == 0 else z_sc[d, bi, 9:c + 9, :]
    zm = z + mu_ref[d:d + 1, :] * (zs - z)
    if d == 0:
      z_sc[d, bi, 7:8, :] = z[c - 1:c, :]
    else:
      z_sc[d, bi, c + 8:c + 9, :] = z[0:1, :]
    wa = zm[:, 768:896]
    logw = -DECAY_SCALE * jax.nn.sigmoid(w0_ref[d:d + 1, :] + _dot(jnp.tanh(wa), w2_ref[d]))
    ag = jax.nn.sigmoid(a0_ref[d:d + 1, :] + _dot(wa, a2_ref[d]))
    tri = jnp.where(upto_c, 1.0, 0.0).astype(BF16)
    cum = sum(jnp.dot(tri, piece, preferred_element_type=F32) for piece in _split3(logw))
    st[ch] = dict(zm=zm, logw=logw, ag=ag, cum=cum)

  def stage0_rest(ch):
    d, bi = ch
    e = st[ch]
    zm, logw, ag, cum = e["zm"], e["logw"], e["ag"], e["cum"]
    r = zm[:, 0:256]
    k = zm[:, 256:512]
    v = zm[:, 512:768]
    kkr = k * kkp
    kkn = kkr * jnp.minimum(lax.rsqrt(_head_sum(kkr * kkr)), 1e12)
    kt = k * (1.0 + (ag - 1.0) * kap)
    bonus = _head_sum(r * kt * rho_ref[d:d + 1, :]) * v
    tot = cum[c - 1:c, :] if d == 0 else cum[0:1, :]
    e_neg = jnp.exp(-cum)
    e_rest = jnp.exp(tot - cum)
    bvec = kkn * ag
    at = -kkn * jnp.exp(cum - logw)
    rt = r * jnp.exp(cum)
    st[ch] = dict(
        v=v, bonus=bonus, g_all=jnp.exp(tot), at=at, rt=rt,
        x=jnp.concatenate([stack(at), stack(rt)], axis=0),
        y=jnp.concatenate([stack(bvec * e_neg), stack(kt * e_neg)], axis=0),
        bk=jnp.concatenate([bvec * e_rest, kt * e_rest], axis=0),
        vm=stack(v))

  def stage1(ch):
    d, bi = ch
    e = st[ch]
    before = (ss < ts) if d == 0 else (ss > ts)
    upto = (ss <= ts) if d == 0 else (ss >= ts)
    a_all = _dot_nt(e["x"], e["y"])
    e["lab"] = jnp.where(before, a_all[0:sr, 0:sr], 0.0)
    lak = jnp.where(before, a_all[0:sr, sr:2 * sr], 0.0)
    e["p"] = jnp.concatenate([jnp.where(upto, a_all[sr:2 * sr, 0:sr], 0.0),
                              jnp.where(upto, a_all[sr:2 * sr, sr:2 * sr], 0.0)], axis=-1)
    s0 = s_sc[d, bi]
    e["s0"] = s0
    ar = _dot_nt(jnp.concatenate([e["at"], e["rt"]], axis=0), s0)
    e["rh"] = ar[c:2 * c]
    e["xs"] = stack(ar[0:c]) + _dot(lak, e["vm"])

  def stage2(group):
    for ch in group:
      e = st[ch]
      e["lp"] = _dot(e["lab"], e["lab"])
      e["t"] = eye + e["lab"]
    for it in range(4):
      for ch in group:
        e = st[ch]
        lp = e["lp"]
        e["t"] = e["t"] + _dot(e["t"], lp)
        if it < 3:
          e["lp"] = _dot(lp, lp)

  def stage3(group):
    for ch in group:
      e = st[ch]
      e["us"] = _dot(e["t"], e["xs"])
    for ch in group:
      d, bi = ch
      e = st[ch]
      ys = _dot(e["p"], jnp.concatenate([e["us"], e["vm"]], axis=0))
      y = e["rh"] + unstack(ys)
      u = unstack(e["us"])
      upd = _dot_tn(jnp.concatenate([u, e["v"]], axis=0), e["bk"])
      s_sc[d, bi] = e["s0"] * e["g_all"] + jnp.where(blockdiag, upd, 0.0)
      y_ref = yf_ref if d == 0 else yb_ref
      for s in range(2):
        sl = slice(s * LANES, (s + 1) * LANES)
        y_ref[bi, :, sl] = (_head_norm_slab(y[:, sl], lo) * lng[0:1, sl] + lnb[0:1, sl]
                            + e["bonus"][:, sl])

  chains = [(d, bi) for d in range(2) for bi in range(nbatch)]
  for ch in chains:
    stage0_mm(ch)
  for ch in chains:
    stage0_rest(ch)
  for ch in chains:
    stage1(ch)
  stage2(chains)
  stage3(chains)


def _rwkv(dd, mu, w0, w2p, a0, a2p, rho, kk, ka, lng, lnb, dims):
  b, n, ct = dims
  c = RWKV_CHUNK
  sa = n + ct
  bpb = sa // c
  ncx = ct // c
  zw = 768 + LANES
  f, r = _scan_maps(bpb, ncx)
  fwd = lambda j: (0, f(j), 0)
  bwd = lambda j: (0, r(j), 0)
  const2 = lambda j: (0, 0)
  const3 = lambda j: (0, 0, 0)
  d3 = dd.reshape(b, sa, D_COLS)

  yf, yb = pl.pallas_call(
      functools.partial(_rwkv_kernel, ncx=ncx, nbatch=b),
      grid=(bpb,),
      in_specs=[
          pl.BlockSpec((b, c, D_COLS), fwd),
          pl.BlockSpec((b, c, D_COLS), bwd),
          pl.BlockSpec((2, zw), const2),
          pl.BlockSpec((2, GROUP_W), const2),
          pl.BlockSpec((2, LANES, GROUP_W), const3),
          pl.BlockSpec((2, GROUP_W), const2),
          pl.BlockSpec((2, LANES, GROUP_W), const3),
          pl.BlockSpec((2, GROUP_W), const2),
          pl.BlockSpec((1, GROUP_W), const2),
          pl.BlockSpec((1, GROUP_W), const2),
          pl.BlockSpec((1, GROUP_W), const2),
          pl.BlockSpec((1, GROUP_W), const2),
      ],
      out_specs=[pl.BlockSpec((b, c, GROUP_W), fwd), pl.BlockSpec((b, c, GROUP_W), bwd)],
      out_shape=[jax.ShapeDtypeStruct((b, sa, GROUP_W), F32)] * 2,
      scratch_shapes=[pltpu.VMEM((2, b, GROUP_W, GROUP_W), F32), pltpu.VMEM((2, b, c + 16, zw), F32)],
      compiler_params=pltpu.CompilerParams(
          dimension_semantics=("arbitrary",), vmem_limit_bytes=VMEM_LIMIT),
      name="rwkv7",
  )(d3, d3, mu, w0, w2p, a0, a2p, rho, kk, ka, lng, lnb)
  return yf.reshape(b * sa, GROUP_W), yb.reshape(b * sa, GROUP_W)


_Q_HEAD_ORDER = (0, 2, 1, 3)


def _q_perm():
  return np.concatenate([np.arange(h * HEAD_DIM, (h + 1) * HEAD_DIM) for h in _Q_HEAD_ORDER])


def _proj_col_perm():
  qp = _q_perm()
  cols = [qp, np.arange(256, 512), 512 + qp, np.arange(768, 1024), np.arange(1024, 2304)]
  d0 = 2304
  cols.append(np.arange(d0, d0 + 768 + 128))
  wdf, wdb, adf, adb = (np.arange(d0 + 896 + 64 * t, d0 + 896 + 64 * (t + 1)) for t in range(4))
  cols += [wdf, adf, wdb, adb]
  return np.concatenate(cols)


def _take_runs(w, perm, axis):
  perm = np.asarray(perm)
  cuts = [0] + [i for i in range(1, len(perm)) if perm[i] != perm[i - 1] + 1] + [len(perm)]
  parts = [lax.slice_in_dim(w, int(perm[a]), int(perm[b - 1]) + 1, axis=axis)
           for a, b in zip(cuts[:-1], cuts[1:])]
  return jnp.concatenate(parts, axis=axis)


def _rope_tables(n, tail):
  def cos_sin(pos, dim):
    inv = 1.0 / (ROPE_BASE ** (np.arange(0, dim, 2, dtype=np.float64) / dim))
    ang = pos.astype(np.float64)[:, None] * inv[None, :]
    return np.cos(ang), np.sin(ang)

  rows = n // GRID_W
  row = np.repeat(np.arange(rows), GRID_W)
  col = np.arange(rows * GRID_W) % GRID_W
  cr, sr = cos_sin(row, HEAD_DIM // 2)
  cc, sc = cos_sin(col, HEAD_DIM // 2)
  cq, sq = cos_sin(np.arange(n), HEAD_DIM)
  cax = np.concatenate([cr, cr, cc, cc] * 2, axis=-1)
  sax = np.concatenate([-sr, sr, -sc, sc] * 2, axis=-1)
  csq = np.concatenate([cq, cq] * 2, axis=-1)
  ssq = np.concatenate([-sq, sq] * 2, axis=-1)
  ones = np.ones((tail, LANES))
  zeros = np.zeros((tail, LANES))
  return tuple(np.concatenate(t).astype(np.float32)
               for t in ([cax, ones], [sax, zeros], [csq, ones], [ssq, zeros]))


def _retention_tables():
  c = RET_CHUNK
  lg = np.log1p(-np.exp2(-5.0 - np.arange(4, dtype=np.float64)))
  idx = np.arange(c, dtype=np.float64)
  rel = idx[:, None] - idx[None, :]
  dfw = np.where(rel[None] >= 0, np.exp(np.maximum(rel, 0.0)[None] * lg[:, None, None]), 0.0)
  dec = np.stack([dfw, np.swapaxes(dfw, 1, 2)])
  lane_lg = np.repeat(lg, HEAD_DIM)[None, :]
  qw_f = np.exp((idx + 1.0)[:, None] * lane_lg)
  kw_f = np.exp((c - 1.0 - idx)[:, None] * lane_lg)
  qw_b = np.exp((c - idx)[:, None] * lane_lg)
  kw_b = np.exp(idx[:, None] * lane_lg)
  cd = np.broadcast_to(np.exp(c * lane_lg), (c, GROUP_W))
  lw = np.stack([np.stack([qw_f, kw_f, cd]), np.stack([qw_b, kw_b, cd])])
  return dec.astype(np.float32), lw.astype(np.float32)


def kernel(x, c, ctx, c_ctx, w_mod, b_mod, norm_g, ffn_w_in, ffn_w_out, w_in, w_out, attn_sink,
           qk_norm_g, ret_norm_g, rwkv_mu, rwkv_w0, rwkv_w2, rwkv_a0, rwkv_a2, rwkv_rho, rwkv_k_k,
           rwkv_k_a, rwkv_g2, rwkv_ln_g, rwkv_ln_b, final_norm_g):
  b, n, d = x.shape
  ct = ctx.shape[1]
  depth = w_mod.shape[0]
  assert d == D_MODEL and b < MOD_ROWS
  assert n % ROW_TILE == 0 and ct % ROW_TILE == 0 and n % KEY_CHUNK == 0 and ct % Q_TILE == 0
  assert n % RET_CHUNK == 0 and ct % RET_CHUNK == 0 and n % GRID_W == 0 and KEY_CHUNK % ct == 0
  dims = (b, n, ct)

  cs = jnp.zeros((MOD_ROWS, d), F32).at[:b].set(c).at[b].set(c_ctx)
  mod = _modulation(cs, w_mod, b_mod).reshape(depth, MOD_ROWS, N_MOD, d)

  tabs = _rope_tables(n, ROW_TILE)
  dec, lw = _retention_tables()
  col_perm = _proj_col_perm()
  q_perm = _q_perm()
  out_perm = np.concatenate([q_perm, 256 + q_perm, np.arange(512, 1024)])
  zpad = jnp.zeros((2, A_RANK, GROUP_W), F32)

  xs = (x.reshape(b * n, d), ctx.reshape(b * ct, d))
  for l in range(depth):
    last = l == depth - 1
    w1a, w2a = ffn_w_in[l, 0].astype(BF16), ffn_w_out[l, 0].astype(BF16)
    w1b, w2b = ffn_w_in[l, 1].astype(BF16), ffn_w_out[l, 1].astype(BF16)
    win = _take_runs(w_in[l], col_perm, 1).astype(BF16)
    wo = _take_runs(w_out[l], out_perm, 0).astype(BF16)
    qkg = jnp.tile(qk_norm_g[l], (1, GROUP_W // HEAD_DIM))
    sink_tab = jnp.broadcast_to((attn_sink[l] * LOG2E)[jnp.asarray(_Q_HEAD_ORDER)][:, None], (4, LANES))
    w2p = jnp.concatenate([rwkv_w2[l], zpad], axis=1)
    a2p = jnp.concatenate([zpad, rwkv_a2[l]], axis=1)

    xs, qa, kva, qb, kvb, rc, gc, dd = _ffn_proj(
        xs, mod[l], norm_g[l], w1a, w2a, win, qkg, tabs, dims)
    oa = _win_attn(qa, kva, sink_tab, dims)
    ob = _glob_attn(qb, kvb, _score_bound(qk_norm_g[l]), dims)
    cf, cb = _retention(rc, gc, dec, lw, ret_norm_g[l][None, :], dims)
    yf, yb = _rwkv(dd, rwkv_mu[l], rwkv_w0[l], w2p, rwkv_a0[l], a2p,
                   rwkv_rho[l].reshape(2, GROUP_W), rwkv_k_k[l][None, :], rwkv_k_a[l][None, :],
                   rwkv_ln_g[l][None, :], rwkv_ln_b[l][None, :], dims)
    xs = _out_ffn(xs, mod[l], norm_g[l], oa, ob, cf, cb, yf, yb, dd, rwkv_g2[l].astype(BF16), wo,
                  w1b, w2b, final_norm_g[None, :], dims, final=last)
  return xs.reshape(b, n, d)
```

```python
import functools

import numpy as np
import jax
import jax.numpy as jnp
from jax import lax
from jax.experimental import pallas as pl
from jax.experimental.pallas import tpu as pltpu

F32 = jnp.float32
BF16 = jnp.bfloat16

D_MODEL = 1024
HEAD_DIM = 64
N_HEADS = 4
GROUP_W = 256
KV_W = 128
GRID_W = 64
BLOCK = 128
D_FF = 2816
A_RANK = 64
G_RANK = 128
N_MOD = 9
ROPE_BASE = 10000.0
RMS_EPS = 1e-6
GN_EPS = 64e-5
DECAY_SCALE = 0.6065306597126334
PROJ_COLS = 3456
D_COLS = 1152
LANES = 128
MOD_ROWS = 8
BF16_ROWS = 16

ROW_TILE = 256
SUB_TILES = 1
Q_TILE = 256
WIN_Q = 2 * BLOCK
KEY_CHUNK = 2048
RET_CHUNK = 256
RWKV_CHUNK = 32
VMEM_LIMIT = 56 * 1024 * 1024

NEG_BIG = -1e30
LOG2E = 1.4426950408889634
BOUND_MARGIN = 1.02
UNDERFLOW_GUARD = 2.0 ** -100
HIGHEST = lax.Precision.HIGHEST


def _dot(a, b):
  return jnp.dot(a.astype(BF16), b.astype(BF16), preferred_element_type=F32)


def _dot_nt(a, b):
  return lax.dot_general(a.astype(BF16), b.astype(BF16), (((1,), (1,)), ((), ())),
                         preferred_element_type=F32)


def _dot_tn(a, b):
  return jnp.dot(a.astype(F32).T.astype(BF16), b.astype(BF16), preferred_element_type=F32)


def _silu(x):
  return x * jax.nn.sigmoid(x)


def _lane_lo(shape):
  return (lax.broadcasted_iota(jnp.int32, shape, len(shape) - 1) % LANES) < HEAD_DIM


def _half_sum(x, lo):
  s_lo = jnp.sum(jnp.where(lo, x, 0.0), axis=-1, keepdims=True)
  s_all = jnp.sum(x, axis=-1, keepdims=True)
  return jnp.where(lo, s_lo, s_all - s_lo)


def _head_sum(x):
  lo = _lane_lo((1, LANES))
  return jnp.concatenate([_half_sum(x[:, o:o + LANES], lo) for o in range(0, GROUP_W, LANES)], axis=-1)


def _head_norm_slab(y, lo):
  mu = _half_sum(y, lo) * (1.0 / HEAD_DIM)
  d = y - mu
  var = _half_sum(d * d, lo) * (1.0 / HEAD_DIM)
  return d * lax.rsqrt(var + GN_EPS)


def _rope_slab(x, c, s, half):
  first = (lax.broadcasted_iota(jnp.int32, x.shape, 1) % (2 * half)) < half
  partner = jnp.where(first, pltpu.roll(x, LANES - half, axis=1), pltpu.roll(x, half, axis=1))
  return x * c + partner * s


def _mod_kernel(c_ref, w_ref, b_ref, o_ref):
  s = _silu(c_ref[...])
  o_ref[...] = jnp.dot(s, w_ref[...], precision=HIGHEST, preferred_element_type=F32) + b_ref[...]


def _modulation(cs, w_mod, b_mod):
  depth = w_mod.shape[0]
  return pl.pallas_call(
      _mod_kernel,
      grid=(depth, N_MOD),
      in_specs=[
          pl.BlockSpec((MOD_ROWS, D_MODEL), lambda l, j: (0, 0)),
          pl.BlockSpec((None, D_MODEL, D_MODEL), lambda l, j: (l, 0, j)),
          pl.BlockSpec((None, 1, D_MODEL), lambda l, j: (l, 0, j)),
      ],
      out_specs=pl.BlockSpec((None, MOD_ROWS, D_MODEL), lambda l, j: (l, 0, j)),
      out_shape=jax.ShapeDtypeStruct((depth, MOD_ROWS, N_MOD * D_MODEL), F32),
      compiler_params=pltpu.CompilerParams(
          dimension_semantics=("arbitrary", "arbitrary"), vmem_limit_bytes=VMEM_LIMIT),
      name="modulation",
  )(cs, w_mod, b_mod.reshape(depth, 1, N_MOD * D_MODEL))


def _rms_mod(x, g, shift, scale):
  h = x * lax.rsqrt(jnp.mean(x * x, axis=-1, keepdims=True) + RMS_EPS) * g
  return h * (1.0 + scale) + shift


def _row_halves(rows):
  return [slice(0, rows // 2), slice(rows // 2, rows)]


def _swiglu(hs, w1_ref, w2_ref):
  hbs = [h.astype(BF16) for h in hs]
  us = [(jnp.dot(hb, w1_ref[:, 0:D_FF], preferred_element_type=F32),
         jnp.dot(hb, w1_ref[:, D_FF:2 * D_FF], preferred_element_type=F32)) for hb in hbs]
  acts = [(_silu(u1) * u2).astype(BF16) for u1, u2 in us]
  return [jnp.dot(a, w2_ref[...], preferred_element_type=F32) for a in acts]


def _head_rms(x, g):
  lo = _lane_lo((1, LANES))
  sq = x * x
  ms = jnp.concatenate([_half_sum(sq[:, o:o + LANES], lo) for o in range(0, x.shape[-1], LANES)],
                       axis=-1) * (1.0 / HEAD_DIM)
  return x * lax.rsqrt(ms + RMS_EPS) * g


def _split3(x):
  hi = x.astype(BF16)
  r1 = x - hi.astype(F32)
  mid = r1.astype(BF16)
  lo = (r1 - mid.astype(F32)).astype(BF16)
  return hi, mid, lo


def _rope_wide(x, c, s, half):
  parts = [_rope_slab(x[:, o:o + LANES], c, s, half) for o in range(0, x.shape[-1], LANES)]
  return parts[0] if len(parts) == 1 else jnp.concatenate(parts, axis=-1)


def _ffn_proj_kernel(*refs, tpb, ctt, split):
  refs = list(refs)
  nx = 2 if split else 1
  x_refs = [refs[g * nx:(g + 1) * nx] for g in range(SUB_TILES)]
  refs = refs[SUB_TILES * nx:]
  mod_refs, refs = refs[:SUB_TILES], refs[SUB_TILES:]
  ng_ref, w1_ref, w2_ref, win_ref, qkg_ref = refs[:5]
  tab_refs = [refs[5 + 4 * g:9 + 4 * g] for g in range(SUB_TILES)]
  (xo_ref, qa_ref, kva_ref, qb_ref, kvb_ref, rc_ref, gc_ref, dd_ref) = refs[5 + 4 * SUB_TILES:]
  tm = xo_ref.shape[0] // SUB_TILES
  ng = ng_ref[...]
  qkg = qkg_ref[...]
  scale = HEAD_DIM ** -0.5
  tiles, xh, ms, tabs = [], [], [], []
  for g, xr in enumerate(x_refs):
    if split:
      is_ctx = (pl.program_id(0) * SUB_TILES + g) % tpb < ctt
      xg = jnp.where(is_ctx, xr[1][...], xr[0][...])
    else:
      xg = xr[0][...]
    parts = [slice(0, tm)] if SUB_TILES > 1 else _row_halves(tm)
    for rs in parts:
      tiles.append(slice(g * tm + rs.start, g * tm + rs.stop))
      xh.append(xg[rs, :])
      ms.append(mod_refs[g][...])
      tabs.append([t[rs, :] for t in tab_refs[g]])
  ffs = _swiglu([_rms_mod(xr, ng[0:1], m[0:1], m[1:2]) for xr, m in zip(xh, ms)], w1_ref, w2_ref)
  x1s = [xr + 0.5 * m[2:3] * ff for xr, m, ff in zip(xh, ms, ffs)]
  hbs = [_rms_mod(x1, ng[1:2], m[3:4], m[4:5]).astype(BF16) for x1, m in zip(x1s, ms)]
  for rs, x1 in zip(tiles, x1s):
    xo_ref[rs, :] = x1

  for rs, hb, (cax, sax, csq, ssq) in zip(tiles, hbs, tabs):
    p = jnp.dot(hb, win_ref[:, 0:512], preferred_element_type=F32)
    qa_ref[rs, :] = (_rope_wide(p[:, 0:256], cax, sax, 16) * (scale * LOG2E)).astype(BF16)
    kva_ref[rs, 0:128] = _rope_wide(p[:, 256:384], cax, sax, 16).astype(BF16)
    kva_ref[rs, 128:256] = p[:, 384:512].astype(BF16)
    p = jnp.dot(hb, win_ref[:, 512:1024], preferred_element_type=F32)
    qn = _head_rms(p[:, 0:256], qkg[0:1, :])
    kn = _head_rms(p[:, 256:384], qkg[1:2, 0:128])
    qb_ref[rs, :] = (_rope_wide(qn, cax, sax, 16) * (scale * LOG2E)).astype(BF16)
    kvb_ref[rs, 0:128] = _rope_wide(kn, cax, sax, 16).astype(BF16)
    kvb_ref[rs, 128:256] = p[:, 384:512].astype(BF16)
    p = jnp.dot(hb, win_ref[:, 1024:2304], preferred_element_type=F32)
    rc_ref[rs, 0:256] = _rope_wide(p[:, 0:256], csq, ssq, 32).astype(BF16)
    rc_ref[rs, 256:512] = (_rope_wide(p[:, 256:512], csq, ssq, 32) * scale).astype(BF16)
    rc_ref[rs, 512:768] = p[:, 512:768].astype(BF16)
    gc_ref[rs, :] = p[:, 768:1280]
    dd_ref[rs, :] = jnp.dot(hb, win_ref[:, 2304:3456], preferred_element_type=F32)


def _ffn_proj(xs, mod, ng, w1, w2, win, qkg, tabs, dims):
  b, n, ct = dims
  rows = b * (n + ct)
  tm = ROW_TILE
  tpb = (n + ct) // tm
  ctt = ct // tm
  ident = n // tm

  assert (rows // tm) % SUB_TILES == 0

  def const(i):
    return (0, 0)

  def tile(g, fn):
    return lambda i: fn(i * SUB_TILES + g)

  def row(t):
    return (t, 0)

  def mod_idx(t):
    return (jnp.where(t % tpb < ctt, b, t // tpb), 0, 0)

  def tab_idx(t):
    return (jnp.where(t % tpb < ctt, ident, t % tpb - ctt), 0)

  def lat_row(t):
    return ((t // tpb) * (n // tm) + jnp.maximum(t % tpb - ctt, 0), 0)

  def ctx_row(t):
    return ((t // tpb) * ctt + jnp.minimum(t % tpb, ctt - 1), 0)

  split = isinstance(xs, tuple)
  x_maps = (lat_row, ctx_row) if split else (row,)
  xs = list(xs) if split else [xs]
  x_specs = [pl.BlockSpec((tm, D_MODEL), tile(g, fn)) for g in range(SUB_TILES) for fn in x_maps]
  mod_specs = [pl.BlockSpec((None, N_MOD, D_MODEL), tile(g, mod_idx)) for g in range(SUB_TILES)]
  tab_specs = [pl.BlockSpec((tm, LANES), tile(g, tab_idx)) for g in range(SUB_TILES) for _ in tabs]

  widths = (D_MODEL, 256, 256, 256, 256, 768, 512, D_COLS)
  dtypes = (F32, BF16, BF16, BF16, BF16, BF16, F32, F32)
  return pl.pallas_call(
      functools.partial(_ffn_proj_kernel, tpb=tpb, ctt=ctt, split=split),
      grid=(rows // (SUB_TILES * tm),),
      in_specs=x_specs + mod_specs + [
          pl.BlockSpec((3, D_MODEL), const),
          pl.BlockSpec((D_MODEL, 2 * D_FF), const),
          pl.BlockSpec((D_FF, D_MODEL), const),
          pl.BlockSpec((D_MODEL, PROJ_COLS), const),
          pl.BlockSpec((2, GROUP_W), const),
      ] + tab_specs,
      out_specs=[pl.BlockSpec((SUB_TILES * tm, w), lambda i: (i, 0)) for w in widths],
      out_shape=[jax.ShapeDtypeStruct((rows, w), dt) for w, dt in zip(widths, dtypes)],
      compiler_params=pltpu.CompilerParams(
          dimension_semantics=("arbitrary",), vmem_limit_bytes=VMEM_LIMIT),
      name="ffn_proj",
  )(*(xs * SUB_TILES), *([mod] * SUB_TILES), ng, w1, w2, win, qkg, *(list(tabs) * SUB_TILES))


def _out_ffn_kernel(x_ref, *refs, final, n_mod):
  mod_refs, refs = refs[:n_mod], refs[n_mod:]
  (ng_ref, oa_ref, ob_ref, cf_ref, cb_ref, yf_ref, yb_ref, gd_ref, g2_ref, wo_ref, w1_ref, w2_ref,
   fg_ref, xo_ref) = refs
  ng = ng_ref[...]
  halves = _row_halves(x_ref.shape[0])
  mods = [r[...] for r in mod_refs] * (2 // n_mod)
  x1s = []
  for rs, m in zip(halves, mods):
    gate = _dot(jax.nn.sigmoid(gd_ref[rs, :]), g2_ref[...])
    od = (yf_ref[rs, :] + yb_ref[rs, :]) * gate
    oc = cf_ref[rs, :] + cb_ref[rs, :]
    y = jnp.dot(oa_ref[rs, :], wo_ref[0:256, :], preferred_element_type=F32)
    y = y + jnp.dot(ob_ref[rs, :], wo_ref[256:512, :], preferred_element_type=F32)
    y = y + jnp.dot(oc.astype(BF16), wo_ref[512:768, :], preferred_element_type=F32)
    y = y + jnp.dot(od.astype(BF16), wo_ref[768:1024, :], preferred_element_type=F32)
    x1s.append(x_ref[rs, :] + m[5:6] * y)
  ffs = _swiglu([_rms_mod(x1, ng[2:3], m[6:7], m[7:8]) for x1, m in zip(x1s, mods)], w1_ref, w2_ref)
  for rs, x1, m, ff in zip(halves, x1s, mods, ffs):
    x2 = x1 + 0.5 * m[8:9] * ff
    if final:
      x2 = x2 * lax.rsqrt(jnp.mean(x2 * x2, axis=-1, keepdims=True) + RMS_EPS) * fg_ref[...]
    xo_ref[rs, :] = x2


def _out_ffn(x, mod, ng, oa, ob, cf, cb, yf, yb, dd, g2, wo, w1, w2, fg, dims, final):
  b, n, ct = dims
  tm = ROW_TILE
  tpb = (n + ct) // tm
  ctt = ct // tm
  if final:
    rt = tm
    grid = (b, n // tm)
    row = lambda bi, j: (bi * tpb + ctt + j, 0)
    out_row = lambda bi, j: (bi * (n // tm) + j, 0)
    mod_specs = [pl.BlockSpec((None, N_MOD, D_MODEL), lambda bi, j: (bi, 0, 0))]
    gd_idx = lambda bi, j: (bi * tpb + ctt + j, 768 // G_RANK)
    out_rows = b * n
  else:
    rt = 2 * tm
    assert (b * tpb) % 2 == 0
    grid = (b * tpb // 2, 1)
    row = lambda i, j: (i, 0)
    out_row = row
    mod_of = lambda t: (jnp.where(t % tpb < ctt, b, t // tpb), 0, 0)
    mod_specs = [pl.BlockSpec((None, N_MOD, D_MODEL), lambda i, j, g=g: mod_of(2 * i + g))
                 for g in range(2)]
    gd_idx = lambda i, j: (i, 768 // G_RANK)
    out_rows = x.shape[0]
  const = lambda bi, j: (0, 0)

  return pl.pallas_call(
      functools.partial(_out_ffn_kernel, final=final, n_mod=len(mod_specs)),
      grid=grid,
      in_specs=[pl.BlockSpec((rt, D_MODEL), row)] + mod_specs + [
          pl.BlockSpec((3, D_MODEL), const),
          pl.BlockSpec((rt, GROUP_W), row),
          pl.BlockSpec((rt, GROUP_W), row),
          pl.BlockSpec((rt, GROUP_W), row),
          pl.BlockSpec((rt, GROUP_W), row),
          pl.BlockSpec((rt, GROUP_W), row),
          pl.BlockSpec((rt, GROUP_W), row),
          pl.BlockSpec((rt, G_RANK), gd_idx),
          pl.BlockSpec((G_RANK, GROUP_W), const),
          pl.BlockSpec((D_MODEL, D_MODEL), const),
          pl.BlockSpec((D_MODEL, 2 * D_FF), const),
          pl.BlockSpec((D_FF, D_MODEL), const),
          pl.BlockSpec((1, D_MODEL), const),
      ],
      out_specs=pl.BlockSpec((rt, D_MODEL), out_row),
      out_shape=jax.ShapeDtypeStruct((out_rows, D_MODEL), F32),
      compiler_params=pltpu.CompilerParams(
          dimension_semantics=("arbitrary", "arbitrary"), vmem_limit_bytes=VMEM_LIMIT),
      name="out_ffn_final" if final else "out_ffn",
  )(x, *([mod] * len(mod_specs)), ng, oa, ob, cf, cb, yf, yb, dd, g2, wo, w1, w2, fg)


def _win_attn_kernel(q_ref, k0_ref, k1_ref, k2_ref, k3_ref, kx_ref, sink_ref, o_ref, *, spb, cs, nb):
  t = pl.program_id(0) % spb
  is_lat = t >= cs
  n = 2 * (t - cs)
  big = 4 * BLOCK
  off_lat = jnp.where(is_lat, 0, 2 * big)
  off0 = jnp.where(jnp.logical_and(is_lat, n > 0), 0, 2 * big)
  off3 = jnp.where(jnp.logical_and(is_lat, n + 2 < nb), 0, 2 * big)
  r = lax.broadcasted_iota(jnp.int32, (BLOCK, WIN_Q), 0)
  lane = lax.broadcasted_iota(jnp.int32, (BLOCK, WIN_Q), 1)
  d = r - lane % BLOCK
  first = lane < BLOCK
  segs = [
      (k0_ref, d >= jnp.where(first, 0, big) + off0),
      (k1_ref, d >= jnp.where(first, -big, 0) + off_lat),
      (k2_ref, -d >= jnp.where(first, 0, -big) + off_lat),
      (k3_ref, -d >= jnp.where(first, big, 0) + off3),
      (kx_ref, None),
  ]
  lo = _lane_lo((1, LANES))
  qms = []
  for s in range(2):
    qs = q_ref[:, s * LANES:(s + 1) * LANES]
    for hf in range(2):
      qms.append(jnp.where(lo if hf == 0 else jnp.logical_not(lo), qs, jnp.zeros_like(qs)))
  all_scores = []
  for qm in qms:
    scs = []
    for ref, valid in segs:
      sc = _dot_nt(ref[:, 0:LANES], qm)
      scs.append(sc if valid is None else jnp.where(valid, sc, NEG_BIG))
    all_scores.append(scs)
  vts = [ref[:, LANES:2 * LANES].astype(F32).T.astype(BF16) for ref, _ in segs]
  outs = []
  for h, scs in enumerate(all_scores):
    hf = h % 2
    snk = sink_ref[h:h + 1, 0:1]
    mx = jnp.maximum(scs[0].max(axis=0, keepdims=True), snk)
    for sc in scs[1:]:
      mx = jnp.maximum(mx, sc.max(axis=0, keepdims=True))
    den = jnp.exp2(snk - mx)
    acc = None
    for sc, vt in zip(scs, vts):
      p = jnp.exp2(sc - mx)
      den = den + p.sum(axis=0, keepdims=True)
      pv = jnp.dot(vt[hf * HEAD_DIM:(hf + 1) * HEAD_DIM, :], p.astype(BF16),
                   preferred_element_type=F32)
      acc = pv if acc is None else acc + pv
    outs.append(acc / den)
  for s in range(2):
    ot = jnp.concatenate([outs[2 * s], outs[2 * s + 1]], axis=0)
    o_ref[:, s * LANES:(s + 1) * LANES] = ot.T.astype(BF16)


def _win_attn(qa, kva, sink_tab, dims):
  b, n, ct = dims
  rows = qa.shape[0]
  nb = n // BLOCK
  cb = ct // BLOCK
  bpb = nb + cb
  spb = (n + ct) // WIN_Q
  cs = ct // WIN_Q

  def key_block(j):
    def idx(i):
      blk = cb + 2 * (i % spb - cs) + j
      return ((i // spb) * bpb + jnp.clip(blk, cb, bpb - 1), 0)
    return idx

  return pl.pallas_call(
      functools.partial(_win_attn_kernel, spb=spb, cs=cs, nb=nb),
      grid=(rows // WIN_Q,),
      in_specs=[pl.BlockSpec((WIN_Q, GROUP_W), lambda i: (i, 0))]
      + [pl.BlockSpec((BLOCK, GROUP_W), key_block(j)) for j in (-1, 0, 1, 2)]
      + [pl.BlockSpec((ct, GROUP_W), lambda i: ((i // spb) * ((n + ct) // ct), 0)),
         pl.BlockSpec((4, LANES), lambda i: (0, 0))],
      out_specs=pl.BlockSpec((WIN_Q, GROUP_W), lambda i: (i, 0)),
      out_shape=jax.ShapeDtypeStruct((rows, GROUP_W), BF16),
      compiler_params=pltpu.CompilerParams(
          dimension_semantics=("arbitrary",), vmem_limit_bytes=VMEM_LIMIT),
      name="window_attn",
  )(qa, kva, kva, kva, kva, kva, sink_tab)


def _glob_attn_kernel(q_ref, kv_ref, bnd_ref, o_ref, m_sc, l_sc, acc_sc, *, ct, ctq, n_chunks):
  lo = _lane_lo((1, LANES))
  qms = []
  for s in range(2):
    qs = q_ref[:, s * LANES:(s + 1) * LANES]
    for hf in range(2):
      qms.append(jnp.where(lo if hf == 0 else jnp.logical_not(lo), qs, jnp.zeros_like(qs)))
  bound = bnd_ref[0:1, 0:1]
  is_ctx = pl.program_id(1) < ctq
  lat_chunks = [(ct + j * KEY_CHUNK, KEY_CHUNK) for j in range(n_chunks)] + [(0, ct)]

  def scores(start, nk):
    k = kv_ref[start:start + nk, 0:LANES]
    return [_dot_nt(k, qm) for qm in qms]

  def v_t(start, nk):
    return kv_ref[pl.ds(start, nk), LANES:2 * LANES].astype(F32).T.astype(BF16)

  def consume(scs, start, nk):
    vt = v_t(start, nk)
    ones = jnp.ones((BF16_ROWS, nk), BF16)
    lhs = [jnp.concatenate([vt[hf * HEAD_DIM:(hf + 1) * HEAD_DIM, :], ones], axis=0) for hf in range(2)]
    for h, sc in enumerate(scs):
      p = jnp.exp2(sc - bound).astype(BF16)
      pv = jnp.dot(lhs[h % 2], p, preferred_element_type=F32)
      l_sc[h] = l_sc[h] + pv[HEAD_DIM:HEAD_DIM + 1]
      acc_sc[h] = acc_sc[h] + pv[0:HEAD_DIM]

  def run(chunks):
    scs = scores(*chunks[0])
    for i, ch in enumerate(chunks):
      nxt = scores(*chunks[i + 1]) if i + 1 < len(chunks) else None
      consume(scs, *ch)
      scs = nxt

  l_sc[...] = jnp.zeros(l_sc.shape, F32)
  acc_sc[...] = jnp.zeros(acc_sc.shape, F32)

  @pl.when(is_ctx)
  def _():
    run([(0, ct)])

  @pl.when(jnp.logical_not(is_ctx))
  def _():
    run(lat_chunks)

  @pl.when(jnp.min(l_sc[...]) < UNDERFLOW_GUARD)
  def _():
    m_sc[...] = jnp.full(m_sc.shape, NEG_BIG, F32)
    l_sc[...] = jnp.zeros(l_sc.shape, F32)
    acc_sc[...] = jnp.zeros(acc_sc.shape, F32)

    def online(start, nk):
      k = kv_ref[pl.ds(start, nk), 0:LANES]
      vt = v_t(start, nk)
      for h, qm in enumerate(qms):
        hf = h % 2
        sc = _dot_nt(k, qm)
        m_prev = m_sc[h]
        m_new = jnp.maximum(m_prev, sc.max(axis=0, keepdims=True))
        alpha = jnp.exp2(m_prev - m_new)
        p = jnp.exp2(sc - m_new)
        l_sc[h] = alpha * l_sc[h] + p.sum(axis=0, keepdims=True)
        pv = jnp.dot(vt[hf * HEAD_DIM:(hf + 1) * HEAD_DIM, :], p.astype(BF16),
                     preferred_element_type=F32)
        acc_sc[h] = alpha * acc_sc[h] + pv
        m_sc[h] = m_new

    def body(j, carry):
      online(pl.multiple_of(ct + j * KEY_CHUNK, ct), KEY_CHUNK)
      return carry

    lax.fori_loop(0, jnp.where(is_ctx, 0, n_chunks), body, 0)
    online(0, ct)

  for s in range(2):
    ot = jnp.concatenate([acc_sc[2 * s] / l_sc[2 * s], acc_sc[2 * s + 1] / l_sc[2 * s + 1]], axis=0)
    o_ref[:, s * LANES:(s + 1) * LANES] = ot.T.astype(BF16)


def _score_bound(qk_g):
  gq = jnp.max(jnp.abs(qk_g[0]))
  gk = jnp.max(jnp.abs(qk_g[1]))
  bound = BOUND_MARGIN * HEAD_DIM * gq * gk * (HEAD_DIM ** -0.5 * LOG2E)
  return jnp.broadcast_to(bound, (MOD_ROWS, LANES)).astype(F32)


def _glob_attn(qb, kvb, bound, dims):
  b, n, ct = dims
  tq = Q_TILE
  sa = n + ct
  return pl.pallas_call(
      functools.partial(_glob_attn_kernel, ct=ct, ctq=ct // tq, n_chunks=n // KEY_CHUNK),
      grid=(b, sa // tq),
      in_specs=[
          pl.BlockSpec((tq, GROUP_W), lambda bi, j: (bi * (sa // tq) + j, 0)),
          pl.BlockSpec((sa, GROUP_W), lambda bi, j: (bi, 0)),
          pl.BlockSpec((MOD_ROWS, LANES), lambda bi, j: (0, 0)),
      ],
      out_specs=pl.BlockSpec((tq, GROUP_W), lambda bi, j: (bi * (sa // tq) + j, 0)),
      out_shape=jax.ShapeDtypeStruct((b * sa, GROUP_W), BF16),
      scratch_shapes=[pltpu.VMEM((4, 1, tq), F32), pltpu.VMEM((4, 1, tq), F32),
                      pltpu.VMEM((4, HEAD_DIM, tq), F32)],
      compiler_params=pltpu.CompilerParams(
          dimension_semantics=("arbitrary", "arbitrary"), vmem_limit_bytes=VMEM_LIMIT),
      name="global_attn",
  )(qb, kvb, bound)


def _ret_kernel(rf_ref, rb_ref, gf_ref, gb_ref, dec_ref, lw_ref, ng_ref, of_ref, ob_ref,
                sf_sc, sb_sc):
  j = pl.program_id(1)

  @pl.when(j == 0)
  def _():
    sf_sc[...] = jnp.zeros(sf_sc.shape, F32)
    sb_sc[...] = jnp.zeros(sb_sc.shape, F32)

  lo = _lane_lo((1, LANES))
  rr = lax.broadcasted_iota(jnp.int32, (LANES, LANES), 0) // HEAD_DIM
  cc = lax.broadcasted_iota(jnp.int32, (LANES, LANES), 1) // HEAD_DIM
  blockdiag = rr == cc
  ng = ng_ref[...]

  for d, (r_ref, g_ref, o_ref, s_sc) in enumerate(
      ((rf_ref, gf_ref, of_ref, sf_sc), (rb_ref, gb_ref, ob_ref, sb_sc))):
    for s in range(2):
      sl = slice(s * LANES, (s + 1) * LANES)
      q = r_ref[:, s * LANES:(s + 1) * LANES]
      k = r_ref[:, 256 + s * LANES:256 + (s + 1) * LANES]
      v = r_ref[:, 512 + s * LANES:512 + (s + 1) * LANES]
      qw = lw_ref[d, 0][:, sl]
      kw = lw_ref[d, 1][:, sl]
      cd = lw_ref[d, 2][0:1, sl]
      st = s_sc[s]
      o = _dot(q.astype(F32) * qw, st)
      halves = []
      for hf in range(2):
        qm = jnp.where(lo if hf == 0 else jnp.logical_not(lo), q, jnp.zeros_like(q))
        att = _dot_nt(qm, k) * dec_ref[d, 2 * s + hf]
        halves.append(_dot(att, v))
      o = o + jnp.where(lo, halves[0], halves[1])
      u = _dot_tn(k.astype(F32) * kw, v)
      s_sc[s] = st * cd + jnp.where(blockdiag, u, 0.0)
      gate = g_ref[:, d * GROUP_W + s * LANES:d * GROUP_W + (s + 1) * LANES]
      o_ref[:, sl] = _head_norm_slab(o, lo) * ng[0:1, sl] * _silu(gate)


def _scan_maps(bpb, ncx):
  def fwd(j):
    return j

  def bwd(j):
    return jnp.where(j < ncx, ncx - 1 - j, bpb - 1 - (j - ncx))

  return fwd, bwd


def _retention(rc, gc, dec, lw, ng, dims):
  b, n, ct = dims
  c = RET_CHUNK
  rows = rc.shape[0]
  bpb = (n + ct) // c
  f, r = _scan_maps(bpb, ct // c)
  fwd = lambda bi, j: (bi * bpb + f(j), 0)
  bwd = lambda bi, j: (bi * bpb + r(j), 0)

  return pl.pallas_call(
      _ret_kernel,
      grid=(b, bpb),
      in_specs=[
          pl.BlockSpec((c, 768), fwd),
          pl.BlockSpec((c, 768), bwd),
          pl.BlockSpec((c, 512), fwd),
          pl.BlockSpec((c, 512), bwd),
          pl.BlockSpec((2, 4, c, c), lambda bi, j: (0, 0, 0, 0)),
          pl.BlockSpec((2, 3, c, GROUP_W), lambda bi, j: (0, 0, 0, 0)),
          pl.BlockSpec((1, GROUP_W), lambda bi, j: (0, 0)),
      ],
      out_specs=[pl.BlockSpec((c, GROUP_W), fwd), pl.BlockSpec((c, GROUP_W), bwd)],
      out_shape=[jax.ShapeDtypeStruct((rows, GROUP_W), F32)] * 2,
      scratch_shapes=[pltpu.VMEM((2, LANES, LANES), F32)] * 2,
      compiler_params=pltpu.CompilerParams(
          dimension_semantics=("arbitrary", "arbitrary"), vmem_limit_bytes=VMEM_LIMIT),
      name="retention",
  )(rc, rc, gc, gc, dec, lw, ng)


def _rwkv_kernel(df_ref, db_ref, mu_ref, w0_ref, w2_ref, a0_ref, a2_ref, rho_ref, kk_ref, ka_ref,
                 lng_ref, lnb_ref, yf_ref, yb_ref, s_sc, z_sc, *, ncx, nbatch):
  j = pl.program_id(0)
  c = RWKV_CHUNK
  sr = N_HEADS * c
  at_start = jnp.logical_or(j == 0, j == ncx)

  @pl.when(j == 0)
  def _():
    s_sc[...] = jnp.zeros(s_sc.shape, F32)

  @pl.when(at_start)
  def _():
    z_sc[:, :, 0:8, :] = jnp.zeros((2, nbatch, 8, z_sc.shape[3]), F32)
    z_sc[:, :, c + 8:c + 16, :] = jnp.zeros((2, nbatch, 8, z_sc.shape[3]), F32)

  lane_head = lax.broadcasted_iota(jnp.int32, (1, GROUP_W), 1) // HEAD_DIM
  head_masks = [lane_head == h for h in range(N_HEADS)]
  rr = lax.broadcasted_iota(jnp.int32, (GROUP_W, GROUP_W), 0) // HEAD_DIM
  cc = lax.broadcasted_iota(jnp.int32, (GROUP_W, GROUP_W), 1) // HEAD_DIM
  blockdiag = rr == cc
  ti = lax.broadcasted_iota(jnp.int32, (c, c), 0)
  si = lax.broadcasted_iota(jnp.int32, (c, c), 1)
  ts = lax.broadcasted_iota(jnp.int32, (sr, sr), 0)
  ss = lax.broadcasted_iota(jnp.int32, (sr, sr), 1)
  eye = jnp.where(ts == ss, 1.0, 0.0)
  ts, ss = ts % c, ss % c
  kkp = kk_ref[...]
  kap = ka_ref[...]
  lng = lng_ref[...]
  lnb = lnb_ref[...]
  lo = _lane_lo((1, LANES))

  def stack(x):
    return jnp.concatenate([jnp.where(m, x, 0.0) for m in head_masks], axis=0)

  def unstack(x):
    return x[0:c] + x[c:2 * c] + x[2 * c:3 * c] + x[3 * c:4 * c]

  st = {}

  def stage0_mm(ch):
    d, bi = ch
    d_ref = df_ref if d == 0 else db_ref
    upto_c = (si <= ti) if d == 0 else (si >= ti)
    z = jnp.concatenate([d_ref[bi, :, 0:768], d_ref[bi, :, 896 + d * LANES:1024 + d * LANES]], axis=-1)
    z_sc[d, bi, 8:c + 8, :] = z
    zs = z_sc[d, bi, 7:c + 7, :] if d == 0 else z_sc[d, bi, 9:c + 9, :]
    zm = z + mu_ref[d:d + 1, :] * (zs - z)
    if d == 0:
      z_sc[d, bi, 7:8, :] = z[c - 1:c, :]
    else:
      z_sc[d, bi, c + 8:c + 9, :] = z[0:1, :]
    wa = zm[:, 768:896]
    logw = -DECAY_SCALE * jax.nn.sigmoid(w0_ref[d:d + 1, :] + _dot(jnp.tanh(wa), w2_ref[d]))
    ag = jax.nn.sigmoid(a0_ref[d:d + 1, :] + _dot(wa, a2_ref[d]))
    tri = jnp.where(upto_c, 1.0, 0.0).astype(BF16)
    cum = sum(jnp.dot(tri, piece, preferred_element_type=F32) for piece in _split3(logw))
    st[ch] = dict(zm=zm, logw=logw, ag=ag, cum=cum)

  def stage0_rest(ch):
    d, bi = ch
    e = st[ch]
    zm, logw, ag, cum = e["zm"], e["logw"], e["ag"], e["cum"]
    r = zm[:, 0:256]
    k = zm[:, 256:512]
    v = zm[:, 512:768]
    kkr = k * kkp
    kkn = kkr * jnp.minimum(lax.rsqrt(_head_sum(kkr * kkr)), 1e12)
    kt = k * (1.0 + (ag - 1.0) * kap)
    bonus = _head_sum(r * kt * rho_ref[d:d + 1, :]) * v
    tot = cum[c - 1:c, :] if d == 0 else cum[0:1, :]
    e_neg = jnp.exp(-cum)
    e_rest = jnp.exp(tot - cum)
    bvec = kkn * ag
    at = -kkn * jnp.exp(cum - logw)
    rt = r * jnp.exp(cum)
    st[ch] = dict(
        v=v, bonus=bonus, g_all=jnp.exp(tot), at=at, rt=rt,
        x=jnp.concatenate([stack(at), stack(rt)], axis=0),
        y=jnp.concatenate([stack(bvec * e_neg), stack(kt * e_neg)], axis=0),
        bk=jnp.concatenate([bvec * e_rest, kt * e_rest], axis=0),
        vm=stack(v))

  def stage1(ch):
    d, bi = ch
    e = st[ch]
    before = (ss < ts) if d == 0 else (ss > ts)
    upto = (ss <= ts) if d == 0 else (ss >= ts)
    a_all = _dot_nt(e["x"], e["y"])
    e["lab"] = jnp.where(before, a_all[0:sr, 0:sr], 0.0)
    lak = jnp.where(before, a_all[0:sr, sr:2 * sr], 0.0)
    e["p"] = jnp.concatenate([jnp.where(upto, a_all[sr:2 * sr, 0:sr], 0.0),
                              jnp.where(upto, a_all[sr:2 * sr, sr:2 * sr], 0.0)], axis=-1)
    s0 = s_sc[d, bi]
    e["s0"] = s0
    ar = _dot_nt(jnp.concatenate([e["at"], e["rt"]], axis=0), s0)
    e["rh"] = ar[c:2 * c]
    e["xs"] = stack(ar[0:c]) + _dot(lak, e["vm"])

  def stage2(group):
    for ch in group:
      e = st[ch]
      e["lp"] = _dot(e["lab"], e["lab"])
      e["t"] = eye + e["lab"]
    for it in range(4):
      for ch in group:
        e = st[ch]
        lp = e["lp"]
        e["t"] = e["t"] + _dot(e["t"], lp)
        if it < 3:
          e["lp"] = _dot(lp, lp)

  def stage3(group):
    for ch in group:
      e = st[ch]
      e["us"] = _dot(e["t"], e["xs"])
    for ch in group:
      d, bi = ch
      e = st[ch]
      ys = _dot(e["p"], jnp.concatenate([e["us"], e["vm"]], axis=0))
      y = e["rh"] + unstack(ys)
      u = unstack(e["us"])
      upd = _dot_tn(jnp.concatenate([u, e["v"]], axis=0), e["bk"])
      s_sc[d, bi] = e["s0"] * e["g_all"] + jnp.where(blockdiag, upd, 0.0)
      y_ref = yf_ref if d == 0 else yb_ref
      for s in range(2):
        sl = slice(s * LANES, (s + 1) * LANES)
        y_ref[bi, :, sl] = (_head_norm_slab(y[:, sl], lo) * lng[0:1, sl] + lnb[0:1, sl]
                            + e["bonus"][:, sl])

  chains = [(d, bi) for d in range(2) for bi in range(nbatch)]
  for ch in chains:
    stage0_mm(ch)
  for ch in chains:
    stage0_rest(ch)
  for ch in chains:
    stage1(ch)
  stage2(chains)
  stage3(chains)


def _rwkv(dd, mu, w0, w2p, a0, a2p, rho, kk, ka, lng, lnb, dims):
  b, n, ct = dims
  c = RWKV_CHUNK
  sa = n + ct
  bpb = sa // c
  ncx = ct // c
  zw = 768 + LANES
  f, r = _scan_maps(bpb, ncx)
  fwd = lambda j: (0, f(j), 0)
  bwd = lambda j: (0, r(j), 0)
  const2 = lambda j: (0, 0)
  const3 = lambda j: (0, 0, 0)
  d3 = dd.reshape(b, sa, D_COLS)

  yf, yb = pl.pallas_call(
      functools.partial(_rwkv_kernel, ncx=ncx, nbatch=b),
      grid=(bpb,),
      in_specs=[
          pl.BlockSpec((b, c, D_COLS), fwd),
          pl.BlockSpec((b, c, D_COLS), bwd),
          pl.BlockSpec((2, zw), const2),
          pl.BlockSpec((2, GROUP_W), const2),
          pl.BlockSpec((2, LANES, GROUP_W), const3),
          pl.BlockSpec((2, GROUP_W), const2),
          pl.BlockSpec((2, LANES, GROUP_W), const3),
          pl.BlockSpec((2, GROUP_W), const2),
          pl.BlockSpec((1, GROUP_W), const2),
          pl.BlockSpec((1, GROUP_W), const2),
          pl.BlockSpec((1, GROUP_W), const2),
          pl.BlockSpec((1, GROUP_W), const2),
      ],
      out_specs=[pl.BlockSpec((b, c, GROUP_W), fwd), pl.BlockSpec((b, c, GROUP_W), bwd)],
      out_shape=[jax.ShapeDtypeStruct((b, sa, GROUP_W), F32)] * 2,
      scratch_shapes=[pltpu.VMEM((2, b, GROUP_W, GROUP_W), F32), pltpu.VMEM((2, b, c + 16, zw), F32)],
      compiler_params=pltpu.CompilerParams(
          dimension_semantics=("arbitrary",), vmem_limit_bytes=VMEM_LIMIT),
      name="rwkv7",
  )(d3, d3, mu, w0, w2p, a0, a2p, rho, kk, ka, lng, lnb)
  return yf.reshape(b * sa, GROUP_W), yb.reshape(b * sa, GROUP_W)


_Q_HEAD_ORDER = (0, 2, 1, 3)


def _q_perm():
  return np.concatenate([np.arange(h * HEAD_DIM, (h + 1) * HEAD_DIM) for h in _Q_HEAD_ORDER])


def _proj_col_perm():
  qp = _q_perm()
  cols = [qp, np.arange(256, 512), 512 + qp, np.arange(768, 1024), np.arange(1024, 2304)]
  d0 = 2304
  cols.append(np.arange(d0, d0 + 768 + 128))
  wdf, wdb, adf, adb = (np.arange(d0 + 896 + 64 * t, d0 + 896 + 64 * (t + 1)) for t in range(4))
  cols += [wdf, adf, wdb, adb]
  return np.concatenate(cols)


def _take_runs(w, perm, axis):
  perm = np.asarray(perm)
  cuts = [0] + [i for i in range(1, len(perm)) if perm[i] != perm[i - 1] + 1] + [len(perm)]
  parts = [lax.slice_in_dim(w, int(perm[a]), int(perm[b - 1]) + 1, axis=axis)
           for a, b in zip(cuts[:-1], cuts[1:])]
  return jnp.concatenate(parts, axis=axis)


def _rope_tables(n, tail):
  def cos_sin(pos, dim):
    inv = 1.0 / (ROPE_BASE ** (np.arange(0, dim, 2, dtype=np.float64) / dim))
    ang = pos.astype(np.float64)[:, None] * inv[None, :]
    return np.cos(ang), np.sin(ang)

  rows = n // GRID_W
  row = np.repeat(np.arange(rows), GRID_W)
  col = np.arange(rows * GRID_W) % GRID_W
  cr, sr = cos_sin(row, HEAD_DIM // 2)
  cc, sc = cos_sin(col, HEAD_DIM // 2)
  cq, sq = cos_sin(np.arange(n), HEAD_DIM)
  cax = np.concatenate([cr, cr, cc, cc] * 2, axis=-1)
  sax = np.concatenate([-sr, sr, -sc, sc] * 2, axis=-1)
  csq = np.concatenate([cq, cq] * 2, axis=-1)
  ssq = np.concatenate([-sq, sq] * 2, axis=-1)
  ones = np.ones((tail, LANES))
  zeros = np.zeros((tail, LANES))
  return tuple(np.concatenate(t).astype(np.float32)
               for t in ([cax, ones], [sax, zeros], [csq, ones], [ssq, zeros]))


def _retention_tables():
  c = RET_CHUNK
  lg = np.log1p(-np.exp2(-5.0 - np.arange(4, dtype=np.float64)))
  idx = np.arange(c, dtype=np.float64)
  rel = idx[:, None] - idx[None, :]
  dfw = np.where(rel[None] >= 0, np.exp(np.maximum(rel, 0.0)[None] * lg[:, None, None]), 0.0)
  dec = np.stack([dfw, np.swapaxes(dfw, 1, 2)])
  lane_lg = np.repeat(lg, HEAD_DIM)[None, :]
  qw_f = np.exp((idx + 1.0)[:, None] * lane_lg)
  kw_f = np.exp((c - 1.0 - idx)[:, None] * lane_lg)
  qw_b = np.exp((c - idx)[:, None] * lane_lg)
  kw_b = np.exp(idx[:, None] * lane_lg)
  cd = np.broadcast_to(np.exp(c * lane_lg), (c, GROUP_W))
  lw = np.stack([np.stack([qw_f, kw_f, cd]), np.stack([qw_b, kw_b, cd])])
  return dec.astype(np.float32), lw.astype(np.float32)


def kernel(x, c, ctx, c_ctx, w_mod, b_mod, norm_g, ffn_w_in, ffn_w_out, w_in, w_out, attn_sink,
           qk_norm_g, ret_norm_g, rwkv_mu, rwkv_w0, rwkv_w2, rwkv_a0, rwkv_a2, rwkv_rho, rwkv_k_k,
           rwkv_k_a, rwkv_g2, rwkv_ln_g, rwkv_ln_b, final_norm_g):
  b, n, d = x.shape
  ct = ctx.shape[1]
  depth = w_mod.shape[0]
  assert d == D_MODEL and b < MOD_ROWS
  assert n % ROW_TILE == 0 and ct % ROW_TILE == 0 and n % KEY_CHUNK == 0 and ct % Q_TILE == 0
  assert n % RET_CHUNK == 0 and ct % RET_CHUNK == 0 and n % GRID_W == 0 and KEY_CHUNK % ct == 0
  dims = (b, n, ct)

  cs = jnp.zeros((MOD_ROWS, d), F32).at[:b].set(c).at[b].set(c_ctx)
  mod = _modulation(cs, w_mod, b_mod).reshape(depth, MOD_ROWS, N_MOD, d)

  tabs = _rope_tables(n, ROW_TILE)
  dec, lw = _retention_tables()
  col_perm = _proj_col_perm()
  q_perm = _q_perm()
  out_perm = np.concatenate([q_perm, 256 + q_perm, np.arange(512, 1024)])
  zpad = jnp.zeros((2, A_RANK, GROUP_W), F32)

  xs = (x.reshape(b * n, d), ctx.reshape(b * ct, d))
  for l in range(depth):
    last = l == depth - 1
    w1a, w2a = ffn_w_in[l, 0].astype(BF16), ffn_w_out[l, 0].astype(BF16)
    w1b, w2b = ffn_w_in[l, 1].astype(BF16), ffn_w_out[l, 1].astype(BF16)
    win = _take_runs(w_in[l], col_perm, 1).astype(BF16)
    wo = _take_runs(w_out[l], out_perm, 0).astype(BF16)
    qkg = jnp.tile(qk_norm_g[l], (1, GROUP_W // HEAD_DIM))
    sink_tab = jnp.broadcast_to((attn_sink[l] * LOG2E)[jnp.asarray(_Q_HEAD_ORDER)][:, None], (4, LANES))
    w2p = jnp.concatenate([rwkv_w2[l], zpad], axis=1)
    a2p = jnp.concatenate([zpad, rwkv_a2[l]], axis=1)

    xs, qa, kva, qb, kvb, rc, gc, dd = _ffn_proj(
        xs, mod[l], norm_g[l], w1a, w2a, win, qkg, tabs, dims)
    oa = _win_attn(qa, kva, sink_tab, dims)
    ob = _glob_attn(qb, kvb, _score_bound(qk_norm_g[l]), dims)
    cf, cb = _retention(rc, gc, dec, lw, ret_norm_g[l][None, :], dims)
    yf, yb = _rwkv(dd, rwkv_mu[l], rwkv_w0[l], w2p, rwkv_a0[l], a2p,
                   rwkv_rho[l].reshape(2, GROUP_W), rwkv_k_k[l][None, :], rwkv_k_a[l][None, :],
                   rwkv_ln_g[l][None, :], rwkv_ln_b[l][None, :], dims)
    xs = _out_ffn(xs, mod[l], norm_g[l], oa, ob, cf, cb, yf, yb, dd, rwkv_g2[l].astype(BF16), wo,
                  w1b, w2b, final_norm_g[None, :], dims, final=last)
  return xs.reshape(b, n, d)
```

```python
import functools

import numpy as np
import jax
import jax.numpy as jnp
from jax import lax
from jax.experimental import pallas as pl
from jax.experimental.pallas import tpu as pltpu

F32 = jnp.float32
BF16 = jnp.bfloat16

D_MODEL = 1024
HEAD_DIM = 64
N_HEADS = 4
GROUP_W = 256
KV_W = 128
GRID_W = 64
BLOCK = 128
D_FF = 2816
A_RANK = 64
G_RANK = 128
N_MOD = 9
ROPE_BASE = 10000.0
RMS_EPS = 1e-6
GN_EPS = 64e-5
DECAY_SCALE = 0.6065306597126334
PROJ_COLS = 3456
D_COLS = 1152
LANES = 128
MOD_ROWS = 8
BF16_ROWS = 16

ROW_TILE = 256
SUB_TILES = 1
Q_TILE = 256
WIN_Q = 2 * BLOCK
KEY_CHUNK = 2048
RET_CHUNK = 256
RWKV_CHUNK = 32
VMEM_LIMIT = 56 * 1024 * 1024

NEG_BIG = -1e30
LOG2E = 1.4426950408889634
BOUND_MARGIN = 1.02
UNDERFLOW_GUARD = 2.0 ** -100
HIGHEST = lax.Precision.HIGHEST


def _dot(a, b):
  return jnp.dot(a.astype(BF16), b.astype(BF16), preferred_element_type=F32)


def _dot_nt(a, b):
  return lax.dot_general(a.astype(BF16), b.astype(BF16), (((1,), (1,)), ((), ())),
                         preferred_element_type=F32)


def _dot_tn(a, b):
  return jnp.dot(a.astype(F32).T.astype(BF16), b.astype(BF16), preferred_element_type=F32)


def _silu(x):
  return x * jax.nn.sigmoid(x)


def _lane_lo(shape):
  return (lax.broadcasted_iota(jnp.int32, shape, len(shape) - 1) % LANES) < HEAD_DIM


def _half_sum(x, lo):
  s_lo = jnp.sum(jnp.where(lo, x, 0.0), axis=-1, keepdims=True)
  s_all = jnp.sum(x, axis=-1, keepdims=True)
  return jnp.where(lo, s_lo, s_all - s_lo)


def _head_sum(x):
  lo = _lane_lo((1, LANES))
  return jnp.concatenate([_half_sum(x[:, o:o + LANES], lo) for o in range(0, GROUP_W, LANES)], axis=-1)


def _head_norm_slab(y, lo):
  mu = _half_sum(y, lo) * (1.0 / HEAD_DIM)
  d = y - mu
  var = _half_sum(d * d, lo) * (1.0 / HEAD_DIM)
  return d * lax.rsqrt(var + GN_EPS)


def _rope_slab(x, c, s, half):
  first = (lax.broadcasted_iota(jnp.int32, x.shape, 1) % (2 * half)) < half
  partner = jnp.where(first, pltpu.roll(x, LANES - half, axis=1), pltpu.roll(x, half, axis=1))
  return x * c + partner * s


def _mod_kernel(c_ref, w_ref, b_ref, o_ref):
  s = _silu(c_ref[...])
  o_ref[...] = jnp.dot(s, w_ref[...], precision=HIGHEST, preferred_element_type=F32) + b_ref[...]


def _modulation(cs, w_mod, b_mod):
  depth = w_mod.shape[0]
  return pl.pallas_call(
      _mod_kernel,
      grid=(depth, N_MOD),
      in_specs=[
          pl.BlockSpec((MOD_ROWS, D_MODEL), lambda l, j: (0, 0)),
          pl.BlockSpec((None, D_MODEL, D_MODEL), lambda l, j: (l, 0, j)),
          pl.BlockSpec((None, 1, D_MODEL), lambda l, j: (l, 0, j)),
      ],
      out_specs=pl.BlockSpec((None, MOD_ROWS, D_MODEL), lambda l, j: (l, 0, j)),
      out_shape=jax.ShapeDtypeStruct((depth, MOD_ROWS, N_MOD * D_MODEL), F32),
      compiler_params=pltpu.CompilerParams(
          dimension_semantics=("arbitrary", "arbitrary"), vmem_limit_bytes=VMEM_LIMIT),
      name="modulation",
  )(cs, w_mod, b_mod.reshape(depth, 1, N_MOD * D_MODEL))


def _rms_mod(x, g, shift, scale):
  h = x * lax.rsqrt(jnp.mean(x * x, axis=-1, keepdims=True) + RMS_EPS) * g
  return h * (1.0 + scale) + shift


def _row_halves(rows):
  return [slice(0, rows // 2), slice(rows // 2, rows)]


def _swiglu(hs, w1_ref, w2_ref):
  hbs = [h.astype(BF16) for h in hs]
  us = [(jnp.dot(hb, w1_ref[:, 0:D_FF], preferred_element_type=F32),
         jnp.dot(hb, w1_ref[:, D_FF:2 * D_FF], preferred_element_type=F32)) for hb in hbs]
  acts = [(_silu(u1) * u2).astype(BF16) for u1, u2 in us]
  return [jnp.dot(a, w2_ref[...], preferred_element_type=F32) for a in acts]


def _head_rms(x, g):
  lo = _lane_lo((1, LANES))
  sq = x * x
  ms = jnp.concatenate([_half_sum(sq[:, o:o + LANES], lo) for o in range(0, x.shape[-1], LANES)],
                       axis=-1) * (1.0 / HEAD_DIM)
  return x * lax.rsqrt(ms + RMS_EPS) * g


def _split3(x):
  hi = x.astype(BF16)
  r1 = x - hi.astype(F32)
  mid = r1.astype(BF16)
  lo = (r1 - mid.astype(F32)).astype(BF16)
  return hi, mid, lo


def _rope_wide(x, c, s, half):
  parts = [_rope_slab(x[:, o:o + LANES], c, s, half) for o in range(0, x.shape[-1], LANES)]
  return parts[0] if len(parts) == 1 else jnp.concatenate(parts, axis=-1)


def _ffn_proj_kernel(*refs, tpb, ctt, split):
  refs = list(refs)
  nx = 2 if split else 1
  x_refs = [refs[g * nx:(g + 1) * nx] for g in range(SUB_TILES)]
  refs = refs[SUB_TILES * nx:]
  mod_refs, refs = refs[:SUB_TILES], refs[SUB_TILES:]
  ng_ref, w1_ref, w2_ref, win_ref, qkg_ref = refs[:5]
  tab_refs = [refs[5 + 4 * g:9 + 4 * g] for g in range(SUB_TILES)]
  (xo_ref, qa_ref, kva_ref, qb_ref, kvb_ref, rc_ref, gc_ref, dd_ref) = refs[5 + 4 * SUB_TILES:]
  tm = xo_ref.shape[0] // SUB_TILES
  ng = ng_ref[...]
  qkg = qkg_ref[...]
  scale = HEAD_DIM ** -0.5
  tiles, xh, ms, tabs = [], [], [], []
  for g, xr in enumerate(x_refs):
    if split:
      is_ctx = (pl.program_id(0) * SUB_TILES + g) % tpb < ctt
      xg = jnp.where(is_ctx, xr[1][...], xr[0][...])
    else:
      xg = xr[0][...]
    parts = [slice(0, tm)] if SUB_TILES > 1 else _row_halves(tm)
    for rs in parts:
      tiles.append(slice(g * tm + rs.start, g * tm + rs.stop))
      xh.append(xg[rs, :])
      ms.append(mod_refs[g][...])
      tabs.append([t[rs, :] for t in tab_refs[g]])
  ffs = _swiglu([_rms_mod(xr, ng[0:1], m[0:1], m[1:2]) for xr, m in zip(xh, ms)], w1_ref, w2_ref)
  x1s = [xr + 0.5 * m[2:3] * ff for xr, m, ff in zip(xh, ms, ffs)]
  hbs = [_rms_mod(x1, ng[1:2], m[3:4], m[4:5]).astype(BF16) for x1, m in zip(x1s, ms)]
  for rs, x1 in zip(tiles, x1s):
    xo_ref[rs, :] = x1

  for rs, hb, (cax, sax, csq, ssq) in zip(tiles, hbs, tabs):
    p = jnp.dot(hb, win_ref[:, 0:512], preferred_element_type=F32)
    qa_ref[rs, :] = (_rope_wide(p[:, 0:256], cax, sax, 16) * (scale * LOG2E)).astype(BF16)
    kva_ref[rs, 0:128] = _rope_wide(p[:, 256:384], cax, sax, 16).astype(BF16)
    kva_ref[rs, 128:256] = p[:, 384:512].astype(BF16)
    p = jnp.dot(hb, win_ref[:, 512:1024], preferred_element_type=F32)
    qn = _head_rms(p[:, 0:256], qkg[0:1, :])
    kn = _head_rms(p[:, 256:384], qkg[1:2, 0:128])
    qb_ref[rs, :] = (_rope_wide(qn, cax, sax, 16) * (scale * LOG2E)).astype(BF16)
    kvb_ref[rs, 0:128] = _rope_wide(kn, cax, sax, 16).astype(BF16)
    kvb_ref[rs, 128:256] = p[:, 384:512].astype(BF16)
    p = jnp.dot(hb, win_ref[:, 1024:2304], preferred_element_type=F32)
    rc_ref[rs, 0:256] = _rope_wide(p[:, 0:256], csq, ssq, 32).astype(BF16)
    rc_ref[rs, 256:512] = (_rope_wide(p[:, 256:512], csq, ssq, 32) * scale).astype(BF16)
    rc_ref[rs, 512:768] = p[:, 512:768].astype(BF16)
    gc_ref[rs, :] = p[:, 768:1280]
    dd_ref[rs, :] = jnp.dot(hb, win_ref[:, 2304:3456], preferred_element_type=F32)


def _ffn_proj(xs, mod, ng, w1, w2, win, qkg, tabs, dims):
  b, n, ct = dims
  rows = b * (n + ct)
  tm = ROW_TILE
  tpb = (n + ct) // tm
  ctt = ct // tm
  ident = n // tm

  assert (rows // tm) % SUB_TILES == 0

  def const(i):
    return (0, 0)

  def tile(g, fn):
    return lambda i: fn(i * SUB_TILES + g)

  def row(t):
    return (t, 0)

  def mod_idx(t):
    return (jnp.where(t % tpb < ctt, b, t // tpb), 0, 0)

  def tab_idx(t):
    return (jnp.where(t % tpb < ctt, ident, t % tpb - ctt), 0)

  def lat_row(t):
    return ((t // tpb) * (n // tm) + jnp.maximum(t % tpb - ctt, 0), 0)

  def ctx_row(t):
    return ((t // tpb) * ctt + jnp.minimum(t % tpb, ctt - 1), 0)

  split = isinstance(xs, tuple)
  x_maps = (lat_row, ctx_row) if split else (row,)
  xs = list(xs) if split else [xs]
  x_specs = [pl.BlockSpec((tm, D_MODEL), tile(g, fn)) for g in range(SUB_TILES) for fn in x_maps]
  mod_specs = [pl.BlockSpec((None, N_MOD, D_MODEL), tile(g, mod_idx)) for g in range(SUB_TILES)]
  tab_specs = [pl.BlockSpec((tm, LANES), tile(g, tab_idx)) for g in range(SUB_TILES) for _ in tabs]

  widths = (D_MODEL, 256, 256, 256, 256, 768, 512, D_COLS)
  dtypes = (F32, BF16, BF16, BF16, BF16, BF16, F32, F32)
  return pl.pallas_call(
      functools.partial(_ffn_proj_kernel, tpb=tpb, ctt=ctt, split=split),
      grid=(rows // (SUB_TILES * tm),),
      in_specs=x_specs + mod_specs + [
          pl.BlockSpec((3, D_MODEL), const),
          pl.BlockSpec((D_MODEL, 2 * D_FF), const),
          pl.BlockSpec((D_FF, D_MODEL), const),
          pl.BlockSpec((D_MODEL, PROJ_COLS), const),
          pl.BlockSpec((2, GROUP_W), const),
      ] + tab_specs,
      out_specs=[pl.BlockSpec((SUB_TILES * tm, w), lambda i: (i, 0)) for w in widths],
      out_shape=[jax.ShapeDtypeStruct((rows, w), dt) for w, dt in zip(widths, dtypes)],
      compiler_params=pltpu.CompilerParams(
          dimension_semantics=("arbitrary",), vmem_limit_bytes=VMEM_LIMIT),
      name="ffn_proj",
  )(*(xs * SUB_TILES), *([mod] * SUB_TILES), ng, w1, w2, win, qkg, *(list(tabs) * SUB_TILES))


def _out_ffn_kernel(x_ref, *refs, final, n_mod):
  mod_refs, refs = refs[:n_mod], refs[n_mod:]
  (ng_ref, oa_ref, ob_ref, cf_ref, cb_ref, yf_ref, yb_ref, gd_ref, g2_ref, wo_ref, w1_ref, w2_ref,
   fg_ref, xo_ref) = refs
  ng = ng_ref[...]
  halves = _row_halves(x_ref.shape[0])
  mods = [r[...] for r in mod_refs] * (2 // n_mod)
  x1s = []
  for rs, m in zip(halves, mods):
    gate = _dot(jax.nn.sigmoid(gd_ref[rs, :]), g2_ref[...])
    od = (yf_ref[rs, :] + yb_ref[rs, :]) * gate
    oc = cf_ref[rs, :] + cb_ref[rs, :]
    y = jnp.dot(oa_ref[rs, :], wo_ref[0:256, :], preferred_element_type=F32)
    y = y + jnp.dot(ob_ref[rs, :], wo_ref[256:512, :], preferred_element_type=F32)
    y = y + jnp.dot(oc.astype(BF16), wo_ref[512:768, :], preferred_element_type=F32)
    y = y + jnp.dot(od.astype(BF16), wo_ref[768:1024, :], preferred_element_type=F32)
    x1s.append(x_ref[rs, :] + m[5:6] * y)
  ffs = _swiglu([_rms_mod(x1, ng[2:3], m[6:7], m[7:8]) for x1, m in zip(x1s, mods)], w1_ref, w2_ref)
  for rs, x1, m, ff in zip(halves, x1s, mods, ffs):
    x2 = x1 + 0.5 * m[8:9] * ff
    if final:
      x2 = x2 * lax.rsqrt(jnp.mean(x2 * x2, axis=-1, keepdims=True) + RMS_EPS) * fg_ref[...]
    xo_ref[rs, :] = x2


def _out_ffn(x, mod, ng, oa, ob, cf, cb, yf, yb, dd, g2, wo, w1, w2, fg, dims, final):
  b, n, ct = dims
  tm = ROW_TILE
  tpb = (n + ct) // tm
  ctt = ct // tm
  if final:
    rt = tm
    grid = (b, n // tm)
    row = lambda bi, j: (bi * tpb + ctt + j, 0)
    out_row = lambda bi, j: (bi * (n // tm) + j, 0)
    mod_specs = [pl.BlockSpec((None, N_MOD, D_MODEL), lambda bi, j: (bi, 0, 0))]
    gd_idx = lambda bi, j: (bi * tpb + ctt + j, 768 // G_RANK)
    out_rows = b * n
  else:
    rt = 2 * tm
    assert (b * tpb) % 2 == 0
    grid = (b * tpb // 2, 1)
    row = lambda i, j: (i, 0)
    out_row = row
    mod_of = lambda t: (jnp.where(t % tpb < ctt, b, t // tpb), 0, 0)
    mod_specs = [pl.BlockSpec((None, N_MOD, D_MODEL), lambda i, j, g=g: mod_of(2 * i + g))
                 for g in range(2)]
    gd_idx = lambda i, j: (i, 768 // G_RANK)
    out_rows = x.shape[0]
  const = lambda bi, j: (0, 0)

  return pl.pallas_call(
      functools.partial(_out_ffn_kernel, final=final, n_mod=len(mod_specs)),
      grid=grid,
      in_specs=[pl.BlockSpec((rt, D_MODEL), row)] + mod_specs + [
          pl.BlockSpec((3, D_MODEL), const),
          pl.BlockSpec((rt, GROUP_W), row),
          pl.BlockSpec((rt, GROUP_W), row),
          pl.BlockSpec((rt, GROUP_W), row),
          pl.BlockSpec((rt, GROUP_W), row),
          pl.BlockSpec((rt, GROUP_W), row),
          pl.BlockSpec((rt, GROUP_W), row),
          pl.BlockSpec((rt, G_RANK), gd_idx),
          pl.BlockSpec((G_RANK, GROUP_W), const),
          pl.BlockSpec((D_MODEL, D_MODEL), const),
          pl.BlockSpec((D_MODEL, 2 * D_FF), const),
          pl.BlockSpec((D_FF, D_MODEL), const),
          pl.BlockSpec((1, D_MODEL), const),
      ],
      out_specs=pl.BlockSpec((rt, D_MODEL), out_row),
      out_shape=jax.ShapeDtypeStruct((out_rows, D_MODEL), F32),
      compiler_params=pltpu.CompilerParams(
          dimension_semantics=("arbitrary", "arbitrary"), vmem_limit_bytes=VMEM_LIMIT),
      name="out_ffn_final" if final else "out_ffn",
  )(x, *([mod] * len(mod_specs)), ng, oa, ob, cf, cb, yf, yb, dd, g2, wo, w1, w2, fg)


def _win_attn_kernel(q_ref, k0_ref, k1_ref, k2_ref, k3_ref, kx_ref, sink_ref, o_ref, *, spb, cs, nb):
  t = pl.program_id(0) % spb
  is_lat = t >= cs
  n = 2 * (t - cs)
  big = 4 * BLOCK
  off_lat = jnp.where(is_lat, 0, 2 * big)
  off0 = jnp.where(jnp.logical_and(is_lat, n > 0), 0, 2 * big)
  off3 = jnp.where(jnp.logical_and(is_lat, n + 2 < nb), 0, 2 * big)
  r = lax.broadcasted_iota(jnp.int32, (BLOCK, WIN_Q), 0)
  lane = lax.broadcasted_iota(jnp.int32, (BLOCK, WIN_Q), 1)
  d = r - lane % BLOCK
  first = lane < BLOCK
  masks = [
      d >= jnp.where(first, 0, big) + off0,
      d >= jnp.where(first, -big, 0) + off_lat,
      -d >= jnp.where(first, 0, -big) + off_lat,
      -d >= jnp.where(first, big, 0) + off3,
  ]
  k_lat = jnp.concatenate([r[...] for r in (k0_ref, k1_ref, k2_ref, k3_ref)], axis=0)
  segs = [(k_lat, jnp.concatenate(masks, axis=0)), (kx_ref[...], None)]
  lo = _lane_lo((1, LANES))
  qms = []
  for s in range(2):
    qs = q_ref[:, s * LANES:(s + 1) * LANES]
    for hf in range(2):
      qms.append(jnp.where(lo if hf == 0 else jnp.logical_not(lo), qs, jnp.zeros_like(qs)))
  all_scores = []
  for qm in qms:
    scs = []
    for ref, valid in segs:
      sc = _dot_nt(ref[:, 0:LANES], qm)
      scs.append(sc if valid is None else jnp.where(valid, sc, NEG_BIG))
    all_scores.append(scs)
  vts = [ref[:, LANES:2 * LANES].astype(F32).T.astype(BF16) for ref, _ in segs]
  outs = []
  for h, scs in enumerate(all_scores):
    hf = h % 2
    snk = sink_ref[h:h + 1, 0:1]
    mx = jnp.maximum(scs[0].max(axis=0, keepdims=True), snk)
    for sc in scs[1:]:
      mx = jnp.maximum(mx, sc.max(axis=0, keepdims=True))
    den = jnp.exp2(snk - mx)
    acc = None
    for sc, vt in zip(scs, vts):
      p = jnp.exp2(sc - mx)
      den = den + p.sum(axis=0, keepdims=True)
      pv = jnp.dot(vt[hf * HEAD_DIM:(hf + 1) * HEAD_DIM, :], p.astype(BF16),
                   preferred_element_type=F32)
      acc = pv if acc is None else acc + pv
    outs.append(acc / den)
  for s in range(2):
    ot = jnp.concatenate([outs[2 * s], outs[2 * s + 1]], axis=0)
    o_ref[:, s * LANES:(s + 1) * LANES] = ot.T.astype(BF16)


def _win_attn(qa, kva, sink_tab, dims):
  b, n, ct = dims
  rows = qa.shape[0]
  nb = n // BLOCK
  cb = ct // BLOCK
  bpb = nb + cb
  spb = (n + ct) // WIN_Q
  cs = ct // WIN_Q

  def key_block(j):
    def idx(i):
      blk = cb + 2 * (i % spb - cs) + j
      return ((i // spb) * bpb + jnp.clip(blk, cb, bpb - 1), 0)
    return idx

  return pl.pallas_call(
      functools.partial(_win_attn_kernel, spb=spb, cs=cs, nb=nb),
      grid=(rows // WIN_Q,),
      in_specs=[pl.BlockSpec((WIN_Q, GROUP_W), lambda i: (i, 0))]
      + [pl.BlockSpec((BLOCK, GROUP_W), key_block(j)) for j in (-1, 0, 1, 2)]
      + [pl.BlockSpec((ct, GROUP_W), lambda i: ((i // spb) * ((n + ct) // ct), 0)),
         pl.BlockSpec((4, LANES), lambda i: (0, 0))],
      out_specs=pl.BlockSpec((WIN_Q, GROUP_W), lambda i: (i, 0)),
      out_shape=jax.ShapeDtypeStruct((rows, GROUP_W), BF16),
      compiler_params=pltpu.CompilerParams(
          dimension_semantics=("arbitrary",), vmem_limit_bytes=VMEM_LIMIT),
      name="window_attn",
  )(qa, kva, kva, kva, kva, kva, sink_tab)


def _glob_attn_kernel(q_ref, kv_ref, bnd_ref, o_ref, m_sc, l_sc, acc_sc, *, ct, ctq, n_chunks):
  lo = _lane_lo((1, LANES))
  qms = []
  for s in range(2):
    qs = q_ref[:, s * LANES:(s + 1) * LANES]
    for hf in range(2):
      qms.append(jnp.where(lo if hf == 0 else jnp.logical_not(lo), qs, jnp.zeros_like(qs)))
  bound = bnd_ref[0:1, 0:1]
  is_ctx = pl.program_id(1) < ctq
  lat_chunks = [(ct + j * KEY_CHUNK, KEY_CHUNK) for j in range(n_chunks)] + [(0, ct)]

  def scores(start, nk):
    k = kv_ref[start:start + nk, 0:LANES]
    return [_dot_nt(k, qm) for qm in qms]

  def v_t(start, nk):
    return kv_ref[pl.ds(start, nk), LANES:2 * LANES].astype(F32).T.astype(BF16)

  def consume(scs, start, nk):
    vt = v_t(start, nk)
    ones = jnp.ones((BF16_ROWS, nk), BF16)
    lhs = [jnp.concatenate([vt[hf * HEAD_DIM:(hf + 1) * HEAD_DIM, :], ones], axis=0) for hf in range(2)]
    for h, sc in enumerate(scs):
      p = jnp.exp2(sc - bound).astype(BF16)
      pv = jnp.dot(lhs[h % 2], p, preferred_element_type=F32)
      l_sc[h] = l_sc[h] + pv[HEAD_DIM:HEAD_DIM + 1]
      acc_sc[h] = acc_sc[h] + pv[0:HEAD_DIM]

  def run(chunks):
    scs = scores(*chunks[0])
    for i, ch in enumerate(chunks):
      nxt = scores(*chunks[i + 1]) if i + 1 < len(chunks) else None
      consume(scs, *ch)
      scs = nxt

  l_sc[...] = jnp.zeros(l_sc.shape, F32)
  acc_sc[...] = jnp.zeros(acc_sc.shape, F32)

  @pl.when(is_ctx)
  def _():
    run([(0, ct)])

  @pl.when(jnp.logical_not(is_ctx))
  def _():
    run(lat_chunks)

  @pl.when(jnp.min(l_sc[...]) < UNDERFLOW_GUARD)
  def _():
    m_sc[...] = jnp.full(m_sc.shape, NEG_BIG, F32)
    l_sc[...] = jnp.zeros(l_sc.shape, F32)
    acc_sc[...] = jnp.zeros(acc_sc.shape, F32)

    def online(start, nk):
      k = kv_ref[pl.ds(start, nk), 0:LANES]
      vt = v_t(start, nk)
      for h, qm in enumerate(qms):
        hf = h % 2
        sc = _dot_nt(k, qm)
        m_prev = m_sc[h]
        m_new = jnp.maximum(m_prev, sc.max(axis=0, keepdims=True))
        alpha = jnp.exp2(m_prev - m_new)
        p = jnp.exp2(sc - m_new)
        l_sc[h] = alpha * l_sc[h] + p.sum(axis=0, keepdims=True)
        pv = jnp.dot(vt[hf * HEAD_DIM:(hf + 1) * HEAD_DIM, :], p.astype(BF16),
                     preferred_element_type=F32)
        acc_sc[h] = alpha * acc_sc[h] + pv
        m_sc[h] = m_new

    def body(j, carry):
      online(pl.multiple_of(ct + j * KEY_CHUNK, ct), KEY_CHUNK)
      return carry

    lax.fori_loop(0, jnp.where(is_ctx, 0, n_chunks), body, 0)
    online(0, ct)

  for s in range(2):
    ot = jnp.concatenate([acc_sc[2 * s] / l_sc[2 * s], acc_sc[2 * s + 1] / l_sc[2 * s + 1]], axis=0)
    o_ref[:, s * LANES:(s + 1) * LANES] = ot.T.astype(BF16)


def _score_bound(qk_g):
  gq = jnp.max(jnp.abs(qk_g[0]))
  gk = jnp.max(jnp.abs(qk_g[1]))
  bound = BOUND_MARGIN * HEAD_DIM * gq * gk * (HEAD_DIM ** -0.5 * LOG2E)
  return jnp.broadcast_to(bound, (MOD_ROWS, LANES)).astype(F32)


def _glob_attn(qb, kvb, bound, dims):
  b, n, ct = dims
  tq = Q_TILE
  sa = n + ct
  return pl.pallas_call(
      functools.partial(_glob_attn_kernel, ct=ct, ctq=ct // tq, n_chunks=n // KEY_CHUNK),
      grid=(b, sa // tq),
      in_specs=[
          pl.BlockSpec((tq, GROUP_W), lambda bi, j: (bi * (sa // tq) + j, 0)),
          pl.BlockSpec((sa, GROUP_W), lambda bi, j: (bi, 0)),
          pl.BlockSpec((MOD_ROWS, LANES), lambda bi, j: (0, 0)),
      ],
      out_specs=pl.BlockSpec((tq, GROUP_W), lambda bi, j: (bi * (sa // tq) + j, 0)),
      out_shape=jax.ShapeDtypeStruct((b * sa, GROUP_W), BF16),
      scratch_shapes=[pltpu.VMEM((4, 1, tq), F32), pltpu.VMEM((4, 1, tq), F32),
                      pltpu.VMEM((4, HEAD_DIM, tq), F32)],
      compiler_params=pltpu.CompilerParams(
          dimension_semantics=("arbitrary", "arbitrary"), vmem_limit_bytes=VMEM_LIMIT),
      name="global_attn",
  )(qb, kvb, bound)


def _ret_kernel(rf_ref, rb_ref, gf_ref, gb_ref, dec_ref, lw_ref, ng_ref, of_ref, ob_ref,
                sf_sc, sb_sc):
  j = pl.program_id(1)

  @pl.when(j == 0)
  def _():
    sf_sc[...] = jnp.zeros(sf_sc.shape, F32)
    sb_sc[...] = jnp.zeros(sb_sc.shape, F32)

  lo = _lane_lo((1, LANES))
  rr = lax.broadcasted_iota(jnp.int32, (LANES, LANES), 0) // HEAD_DIM
  cc = lax.broadcasted_iota(jnp.int32, (LANES, LANES), 1) // HEAD_DIM
  blockdiag = rr == cc
  ng = ng_ref[...]

  for d, (r_ref, g_ref, o_ref, s_sc) in enumerate(
      ((rf_ref, gf_ref, of_ref, sf_sc), (rb_ref, gb_ref, ob_ref, sb_sc))):
    for s in range(2):
      sl = slice(s * LANES, (s + 1) * LANES)
      q = r_ref[:, s * LANES:(s + 1) * LANES]
      k = r_ref[:, 256 + s * LANES:256 + (s + 1) * LANES]
      v = r_ref[:, 512 + s * LANES:512 + (s + 1) * LANES]
      qw = lw_ref[d, 0][:, sl]
      kw = lw_ref[d, 1][:, sl]
      cd = lw_ref[d, 2][0:1, sl]
      st = s_sc[s]
      o = _dot(q.astype(F32) * qw, st)
      halves = []
      for hf in range(2):
        qm = jnp.where(lo if hf == 0 else jnp.logical_not(lo), q, jnp.zeros_like(q))
        att = _dot_nt(qm, k) * dec_ref[d, 2 * s + hf]
        halves.append(_dot(att, v))
      o = o + jnp.where(lo, halves[0], halves[1])
      u = _dot_tn(k.astype(F32) * kw, v)
      s_sc[s] = st * cd + jnp.where(blockdiag, u, 0.0)
      gate = g_ref[:, d * GROUP_W + s * LANES:d * GROUP_W + (s + 1) * LANES]
      o_ref[:, sl] = _head_norm_slab(o, lo) * ng[0:1, sl] * _silu(gate)


def _scan_maps(bpb, ncx):
  def fwd(j):
    return j

  def bwd(j):
    return jnp.where(j < ncx, ncx - 1 - j, bpb - 1 - (j - ncx))

  return fwd, bwd


def _retention(rc, gc, dec, lw, ng, dims):
  b, n, ct = dims
  c = RET_CHUNK
  rows = rc.shape[0]
  bpb = (n + ct) // c
  f, r = _scan_maps(bpb, ct // c)
  fwd = lambda bi, j: (bi * bpb + f(j), 0)
  bwd = lambda bi, j: (bi * bpb + r(j), 0)

  return pl.pallas_call(
      _ret_kernel,
      grid=(b, bpb),
      in_specs=[
          pl.BlockSpec((c, 768), fwd),
          pl.BlockSpec((c, 768), bwd),
          pl.BlockSpec((c, 512), fwd),
          pl.BlockSpec((c, 512), bwd),
          pl.BlockSpec((2, 4, c, c), lambda bi, j: (0, 0, 0, 0)),
          pl.BlockSpec((2, 3, c, GROUP_W), lambda bi, j: (0, 0, 0, 0)),
          pl.BlockSpec((1, GROUP_W), lambda bi, j: (0, 0)),
      ],
      out_specs=[pl.BlockSpec((c, GROUP_W), fwd), pl.BlockSpec((c, GROUP_W), bwd)],
      out_shape=[jax.ShapeDtypeStruct((rows, GROUP_W), F32)] * 2,
      scratch_shapes=[pltpu.VMEM((2, LANES, LANES), F32)] * 2,
      compiler_params=pltpu.CompilerParams(
          dimension_semantics=("arbitrary", "arbitrary"), vmem_limit_bytes=VMEM_LIMIT),
      name="retention",
  )(rc, rc, gc, gc, dec, lw, ng)


def _rwkv_kernel(df_ref, db_ref, mu_ref, w0_ref, w2_ref, a0_ref, a2_ref, rho_ref, kk_ref, ka_ref,
                 lng_ref, lnb_ref, yf_ref, yb_ref, s_sc, z_sc, *, ncx, nbatch):
  j = pl.program_id(0)
  c = RWKV_CHUNK
  sr = N_HEADS * c
  at_start = jnp.logical_or(j == 0, j == ncx)

  @pl.when(j == 0)
  def _():
    s_sc[...] = jnp.zeros(s_sc.shape, F32)

  @pl.when(at_start)
  def _():
    z_sc[:, :, 0:8, :] = jnp.zeros((2, nbatch, 8, z_sc.shape[3]), F32)
    z_sc[:, :, c + 8:c + 16, :] = jnp.zeros((2, nbatch, 8, z_sc.shape[3]), F32)

  lane_head = lax.broadcasted_iota(jnp.int32, (1, GROUP_W), 1) // HEAD_DIM
  head_masks = [lane_head == h for h in range(N_HEADS)]
  rr = lax.broadcasted_iota(jnp.int32, (GROUP_W, GROUP_W), 0) // HEAD_DIM
  cc = lax.broadcasted_iota(jnp.int32, (GROUP_W, GROUP_W), 1) // HEAD_DIM
  blockdiag = rr == cc
  ti = lax.broadcasted_iota(jnp.int32, (c, c), 0)
  si = lax.broadcasted_iota(jnp.int32, (c, c), 1)
  ts = lax.broadcasted_iota(jnp.int32, (sr, sr), 0)
  ss = lax.broadcasted_iota(jnp.int32, (sr, sr), 1)
  eye = jnp.where(ts == ss, 1.0, 0.0)
  ts, ss = ts % c, ss % c
  kkp = kk_ref[...]
  kap = ka_ref[...]
  lng = lng_ref[...]
  lnb = lnb_ref[...]
  lo = _lane_lo((1, LANES))

  def stack(x):
    return jnp.concatenate([jnp.where(m, x, 0.0) for m in head_masks], axis=0)

  def unstack(x):
    return x[0:c] + x[c:2 * c] + x[2 * c:3 * c] + x[3 * c:4 * c]

  st = {}

  def stage0_mm(ch):
    d, bi = ch
    d_ref = df_ref if d == 0 else db_ref
    upto_c = (si <= ti) if d == 0 else (si >= ti)
    z = jnp.concatenate([d_ref[bi, :, 0:768], d_ref[bi, :, 896 + d * LANES:1024 + d * LANES]], axis=-1)
    z_sc[d, bi, 8:c + 8, :] = z
    zs = z_sc[d, bi, 7:c + 7, :] if d == 0 else z_sc[d, bi, 9:c + 9, :]
    zm = z + mu_ref[d:d + 1, :] * (zs - z)
    if d == 0:
      z_sc[d, bi, 7:8, :] = z[c - 1:c, :]
    else:
      z_sc[d, bi, c + 8:c + 9, :] = z[0:1, :]
    wa = zm[:, 768:896]
    logw = -DECAY_SCALE * jax.nn.sigmoid(w0_ref[d:d + 1, :] + _dot(jnp.tanh(wa), w2_ref[d]))
    ag = jax.nn.sigmoid(a0_ref[d:d + 1, :] + _dot(wa, a2_ref[d]))
    tri = jnp.where(upto_c, 1.0, 0.0).astype(BF16)
    cum = sum(jnp.dot(tri, piece, preferred_element_type=F32) for piece in _split3(logw))
    st[ch] = dict(zm=zm, logw=logw, ag=ag, cum=cum)

  def stage0_rest(ch):
    d, bi = ch
    e = st[ch]
    zm, logw, ag, cum = e["zm"], e["logw"], e["ag"], e["cum"]
    r = zm[:, 0:256]
    k = zm[:, 256:512]
    v = zm[:, 512:768]
    kkr = k * kkp
    kkn = kkr * jnp.minimum(lax.rsqrt(_head_sum(kkr * kkr)), 1e12)
    kt = k * (1.0 + (ag - 1.0) * kap)
    bonus = _head_sum(r * kt * rho_ref[d:d + 1, :]) * v
    tot = cum[c - 1:c, :] if d == 0 else cum[0:1, :]
    e_neg = jnp.exp(-cum)
    e_rest = jnp.exp(tot - cum)
    bvec = kkn * ag
    at = -kkn * jnp.exp(cum - logw)
    rt = r * jnp.exp(cum)
    st[ch] = dict(
        v=v, bonus=bonus, g_all=jnp.exp(tot), at=at, rt=rt,
        x=jnp.concatenate([stack(at), stack(rt)], axis=0),
        y=jnp.concatenate([stack(bvec * e_neg), stack(kt * e_neg)], axis=0),
        bk=jnp.concatenate([bvec * e_rest, kt * e_rest], axis=0),
        vm=stack(v))

  def stage1(ch):
    d, bi = ch
    e = st[ch]
    before = (ss < ts) if d == 0 else (ss > ts)
    upto = (ss <= ts) if d == 0 else (ss >= ts)
    a_all = _dot_nt(e["x"], e["y"])
    e["lab"] = jnp.where(before, a_all[0:sr, 0:sr], 0.0)
    lak = jnp.where(before, a_all[0:sr, sr:2 * sr], 0.0)
    e["p"] = jnp.concatenate([jnp.where(upto, a_all[sr:2 * sr, 0:sr], 0.0),
                              jnp.where(upto, a_all[sr:2 * sr, sr:2 * sr], 0.0)], axis=-1)
    s0 = s_sc[d, bi]
    e["s0"] = s0
    ar = _dot_nt(jnp.concatenate([e["at"], e["rt"]], axis=0), s0)
    e["rh"] = ar[c:2 * c]
    e["xs"] = stack(ar[0:c]) + _dot(lak, e["vm"])

  def stage2(group):
    for ch in group:
      e = st[ch]
      e["lp"] = _dot(e["lab"], e["lab"])
      e["t"] = eye + e["lab"]
    for it in range(4):
      for ch in group:
        e = st[ch]
        lp = e["lp"]
        e["t"] = e["t"] + _dot(e["t"], lp)
        if it < 3:
          e["lp"] = _dot(lp, lp)

  def stage3(group):
    for ch in group:
      e = st[ch]
      e["us"] = _dot(e["t"], e["xs"])
    for ch in group:
      d, bi = ch
      e = st[ch]
      ys = _dot(e["p"], jnp.concatenate([e["us"], e["vm"]], axis=0))
      y = e["rh"] + unstack(ys)
      u = unstack(e["us"])
      upd = _dot_tn(jnp.concatenate([u, e["v"]], axis=0), e["bk"])
      s_sc[d, bi] = e["s0"] * e["g_all"] + jnp.where(blockdiag, upd, 0.0)
      y_ref = yf_ref if d == 0 else yb_ref
      for s in range(2):
        sl = slice(s * LANES, (s + 1) * LANES)
        y_ref[bi, :, sl] = (_head_norm_slab(y[:, sl], lo) * lng[0:1, sl] + lnb[0:1, sl]
                            + e["bonus"][:, sl])

  chains = [(d, bi) for d in range(2) for bi in range(nbatch)]
  for ch in chains:
    stage0_mm(ch)
  for ch in chains:
    stage0_rest(ch)
  for ch in chains:
    stage1(ch)
  stage2(chains)
  stage3(chains)


def _rwkv(dd, mu, w0, w2p, a0, a2p, rho, kk, ka, lng, lnb, dims):
  b, n, ct = dims
  c = RWKV_CHUNK
  sa = n + ct
  bpb = sa // c
  ncx = ct // c
  zw = 768 + LANES
  f, r = _scan_maps(bpb, ncx)
  fwd = lambda j: (0, f(j), 0)
  bwd = lambda j: (0, r(j), 0)
  const2 = lambda j: (0, 0)
  const3 = lambda j: (0, 0, 0)
  d3 = dd.reshape(b, sa, D_COLS)

  yf, yb = pl.pallas_call(
      functools.partial(_rwkv_kernel, ncx=ncx, nbatch=b),
      grid=(bpb,),
      in_specs=[
          pl.BlockSpec((b, c, D_COLS), fwd),
          pl.BlockSpec((b, c, D_COLS), bwd),
          pl.BlockSpec((2, zw), const2),
          pl.BlockSpec((2, GROUP_W), const2),
          pl.BlockSpec((2, LANES, GROUP_W), const3),
          pl.BlockSpec((2, GROUP_W), const2),
          pl.BlockSpec((2, LANES, GROUP_W), const3),
          pl.BlockSpec((2, GROUP_W), const2),
          pl.BlockSpec((1, GROUP_W), const2),
          pl.BlockSpec((1, GROUP_W), const2),
          pl.BlockSpec((1, GROUP_W), const2),
          pl.BlockSpec((1, GROUP_W), const2),
      ],
      out_specs=[pl.BlockSpec((b, c, GROUP_W), fwd), pl.BlockSpec((b, c, GROUP_W), bwd)],
      out_shape=[jax.ShapeDtypeStruct((b, sa, GROUP_W), F32)] * 2,
      scratch_shapes=[pltpu.VMEM((2, b, GROUP_W, GROUP_W), F32), pltpu.VMEM((2, b, c + 16, zw), F32)],
      compiler_params=pltpu.CompilerParams(
          dimension_semantics=("arbitrary",), vmem_limit_bytes=VMEM_LIMIT),
      name="rwkv7",
  )(d3, d3, mu, w0, w2p, a0, a2p, rho, kk, ka, lng, lnb)
  return yf.reshape(b * sa, GROUP_W), yb.reshape(b * sa, GROUP_W)


_Q_HEAD_ORDER = (0, 2, 1, 3)


def _q_perm():
  return np.concatenate([np.arange(h * HEAD_DIM, (h + 1) * HEAD_DIM) for h in _Q_HEAD_ORDER])


def _proj_col_perm():
  qp = _q_perm()
  cols = [qp, np.arange(256, 512), 512 + qp, np.arange(768, 1024), np.arange(1024, 2304)]
  d0 = 2304
  cols.append(np.arange(d0, d0 + 768 + 128))
  wdf, wdb, adf, adb = (np.arange(d0 + 896 + 64 * t, d0 + 896 + 64 * (t + 1)) for t in range(4))
  cols += [wdf, adf, wdb, adb]
  return np.concatenate(cols)


def _take_runs(w, perm, axis):
  perm = np.asarray(perm)
  cuts = [0] + [i for i in range(1, len(perm)) if perm[i] != perm[i - 1] + 1] + [len(perm)]
  parts = [lax.slice_in_dim(w, int(perm[a]), int(perm[b - 1]) + 1, axis=axis)
           for a, b in zip(cuts[:-1], cuts[1:])]
  return jnp.concatenate(parts, axis=axis)


def _rope_tables(n, tail):
  def cos_sin(pos, dim):
    inv = 1.0 / (ROPE_BASE ** (np.arange(0, dim, 2, dtype=np.float64) / dim))
    ang = pos.astype(np.float64)[:, None] * inv[None, :]
    return np.cos(ang), np.sin(ang)

  rows = n // GRID_W
  row = np.repeat(np.arange(rows), GRID_W)
  col = np.arange(rows * GRID_W) % GRID_W
  cr, sr = cos_sin(row, HEAD_DIM // 2)
  cc, sc = cos_sin(col, HEAD_DIM // 2)
  cq, sq = cos_sin(np.arange(n), HEAD_DIM)
  cax = np.concatenate([cr, cr, cc, cc] * 2, axis=-1)
  sax = np.concatenate([-sr, sr, -sc, sc] * 2, axis=-1)
  csq = np.concatenate([cq, cq] * 2, axis=-1)
  ssq = np.concatenate([-sq, sq] * 2, axis=-1)
  ones = np.ones((tail, LANES))
  zeros = np.zeros((tail, LANES))
  return tuple(np.concatenate(t).astype(np.float32)
               for t in ([cax, ones], [sax, zeros], [csq, ones], [ssq, zeros]))


def _retention_tables():
  c = RET_CHUNK
  lg = np.log1p(-np.exp2(-5.0 - np.arange(4, dtype=np.float64)))
  idx = np.arange(c, dtype=np.float64)
  rel = idx[:, None] - idx[None, :]
  dfw = np.where(rel[None] >= 0, np.exp(np.maximum(rel, 0.0)[None] * lg[:, None, None]), 0.0)
  dec = np.stack([dfw, np.swapaxes(dfw, 1, 2)])
  lane_lg = np.repeat(lg, HEAD_DIM)[None, :]
  qw_f = np.exp((idx + 1.0)[:, None] * lane_lg)
  kw_f = np.exp((c - 1.0 - idx)[:, None] * lane_lg)
  qw_b = np.exp((c - idx)[:, None] * lane_lg)
  kw_b = np.exp(idx[:, None] * lane_lg)
  cd = np.broadcast_to(np.exp(c * lane_lg), (c, GROUP_W))
  lw = np.stack([np.stack([qw_f, kw_f, cd]), np.stack([qw_b, kw_b, cd])])
  return dec.astype(np.float32), lw.astype(np.float32)


def kernel(x, c, ctx, c_ctx, w_mod, b_mod, norm_g, ffn_w_in, ffn_w_out, w_in, w_out, attn_sink,
           qk_norm_g, ret_norm_g, rwkv_mu, rwkv_w0, rwkv_w2, rwkv_a0, rwkv_a2, rwkv_rho, rwkv_k_k,
           rwkv_k_a, rwkv_g2, rwkv_ln_g, rwkv_ln_b, final_norm_g):
  b, n, d = x.shape
  ct = ctx.shape[1]
  depth = w_mod.shape[0]
  assert d == D_MODEL and b < MOD_ROWS
  assert n % ROW_TILE == 0 and ct % ROW_TILE == 0 and n % KEY_CHUNK == 0 and ct % Q_TILE == 0
  assert n % RET_CHUNK == 0 and ct % RET_CHUNK == 0 and n % GRID_W == 0 and KEY_CHUNK % ct == 0
  dims = (b, n, ct)

  cs = jnp.zeros((MOD_ROWS, d), F32).at[:b].set(c).at[b].set(c_ctx)
  mod = _modulation(cs, w_mod, b_mod).reshape(depth, MOD_ROWS, N_MOD, d)

  tabs = _rope_tables(n, ROW_TILE)
  dec, lw = _retention_tables()
  col_perm = _proj_col_perm()
  q_perm = _q_perm()
  out_perm = np.concatenate([q_perm, 256 + q_perm, np.arange(512, 1024)])
  zpad = jnp.zeros((2, A_RANK, GROUP_W), F32)

  xs = (x.reshape(b * n, d), ctx.reshape(b * ct, d))
  for l in range(depth):
    last = l == depth - 1
    w1a, w2a = ffn_w_in[l, 0].astype(BF16), ffn_w_out[l, 0].astype(BF16)
    w1b, w2b = ffn_w_in[l, 1].astype(BF16), ffn_w_out[l, 1].astype(BF16)
    win = _take_runs(w_in[l], col_perm, 1).astype(BF16)
    wo = _take_runs(w_out[l], out_perm, 0).astype(BF16)
    qkg = jnp.tile(qk_norm_g[l], (1, GROUP_W // HEAD_DIM))
    sink_tab = jnp.broadcast_to((attn_sink[l] * LOG2E)[jnp.asarray(_Q_HEAD_ORDER)][:, None], (4, LANES))
    w2p = jnp.concatenate([rwkv_w2[l], zpad], axis=1)
    a2p = jnp.concatenate([zpad, rwkv_a2[l]], axis=1)

    xs, qa, kva, qb, kvb, rc, gc, dd = _ffn_proj(
        xs, mod[l], norm_g[l], w1a, w2a, win, qkg, tabs, dims)
    oa = _win_attn(qa, kva, sink_tab, dims)
    ob = _glob_attn(qb, kvb, _score_bound(qk_norm_g[l]), dims)
    cf, cb = _retention(rc, gc, dec, lw, ret_norm_g[l][None, :], dims)
    yf, yb = _rwkv(dd, rwkv_mu[l], rwkv_w0[l], w2p, rwkv_a0[l], a2p,
                   rwkv_rho[l].reshape(2, GROUP_W), rwkv_k_k[l][None, :], rwkv_k_a[l][None, :],
                   rwkv_ln_g[l][None, :], rwkv_ln_b[l][None, :], dims)
    xs = _out_ffn(xs, mod[l], norm_g[l], oa, ob, cf, cb, yf, yb, dd, rwkv_g2[l].astype(BF16), wo,
                  w1b, w2b, final_norm_g[None, :], dims, final=last)
  return xs.reshape(b, n, d)
```
